```python
import jax, jax.numpy as jnp
from jax import lax
import numpy as np

D_MODEL = 1024
BATCH = 16
SEQ = 4096
DEPTH = 2

N_MIXERS = 2
N_SSD_LAYERS = (DEPTH + 1) // 2
N_HGRN_LAYERS = DEPTH // 2
NORM_EPS = 1e-6

SSD_EXPAND = 2
SSD_D_INNER = SSD_EXPAND * D_MODEL
SSD_HEAD_DIM = 64
SSD_N_HEADS = SSD_D_INNER // SSD_HEAD_DIM
SSD_N_GROUPS = 4
SSD_HEADS_PER_GROUP = SSD_N_HEADS // SSD_N_GROUPS
SSD_D_STATE = 128
SSD_CONV_WIDTH = 4
SSD_CONV_DIM = SSD_D_INNER + 2 * SSD_N_GROUPS * SSD_D_STATE
SSD_IN_DIM = SSD_D_INNER + SSD_CONV_DIM + SSD_N_HEADS
SSD_NORM_GROUP = SSD_D_INNER // SSD_N_GROUPS
SSD_CHUNK = 128

HGRN_EXPAND = 128
HGRN_N_HEADS = D_MODEL // HGRN_EXPAND
HGRN_K_DIM = HGRN_EXPAND
HGRN_V_DIM = D_MODEL // HGRN_N_HEADS
HGRN_FORGET_DIM = HGRN_N_HEADS * HGRN_K_DIM
HGRN_IN_DIM = 2 * HGRN_FORGET_DIM + 2 * D_MODEL
HGRN_CHUNK = 64

PEER_N_KEYS = 128
PEER_N_EXPERTS = PEER_N_KEYS * PEER_N_KEYS
PEER_N_HEADS = 8
PEER_D_QUERY = 256
PEER_D_HALF = PEER_D_QUERY // 2
PEER_TOPK = 16
PEER_TOKEN_BLOCK = 128

kernel_name = 'hybrid_ssd_hgrn2_peer'


def rmsnorm(x, g):
    xf = x.astype(jnp.float32)
    r = lax.rsqrt(jnp.mean(xf * xf, axis=-1, keepdims=True) + NORM_EPS)
    return (xf * r).astype(x.dtype) * g


def to_chunks(t, length):
    b, s = t.shape[0], t.shape[1]
    return jnp.moveaxis(t.reshape((b, s // length, length) + t.shape[2:]), 1, 0)


def from_chunks(t):
    t = jnp.moveaxis(t, 0, 1)
    return t.reshape((t.shape[0], t.shape[1] * t.shape[2]) + t.shape[3:])


def causal_depthwise_conv(x, w, b):
    c = x.shape[-1]
    y = lax.conv_general_dilated(x, w[:, None, :].astype(x.dtype), window_strides=(1,),
                                 padding=[(SSD_CONV_WIDTH - 1, 0)],
                                 dimension_numbers=('NWC', 'WIO', 'NWC'),
                                 feature_group_count=c)
    return y + b


def ssd_chunked_scan(x, dt, a, bm, cm):
    bsz, _, g, hg, p = x.shape
    n = bm.shape[-1]
    mask = jnp.tril(jnp.ones((SSD_CHUNK, SSD_CHUNK), bool))[None, :, :, None, None]

    def step(state, inp):
        xc, dtc, bc, cc = inp
        acs = jnp.cumsum(dtc * a, axis=1)
        seg = acs[:, :, None] - acs[:, None, :]
        decay = jnp.exp(jnp.where(mask, seg, -jnp.inf))
        cb = jnp.einsum('blgn,bsgn->blsg', cc, bc)
        w = cb[..., None] * decay * dtc[:, None]
        y = jnp.einsum('blsgh,bsghp->blghp', w, xc)
        y = y + jnp.einsum('blgn,bghpn->blghp', cc, state) * jnp.exp(acs)[..., None]
        tail = jnp.exp(acs[:, -1:] - acs) * dtc
        state = (state * jnp.exp(acs[:, -1])[..., None, None]
                 + jnp.einsum('bsgh,bsgn,bsghp->bghpn', tail, bc, xc))
        return state, y

    state0 = jnp.zeros((bsz, g, hg, p, n), jnp.float32)
    _, ys = lax.scan(step, state0, (to_chunks(x, SSD_CHUNK), to_chunks(dt, SSD_CHUNK),
                                    to_chunks(bm, SSD_CHUNK), to_chunks(cm, SSD_CHUNK)))
    return from_chunks(ys)


def ssd_mixer(h, w_in, conv_w, conv_b, dt_bias, a_log, d_skip, norm_g, w_out):
    bsz, s, _ = h.shape
    g, hg, p, n = SSD_N_GROUPS, SSD_HEADS_PER_GROUP, SSD_HEAD_DIM, SSD_D_STATE
    z, xbc, dt = jnp.split(h @ w_in, [SSD_D_INNER, SSD_D_INNER + SSD_CONV_DIM], axis=-1)
    xbc = jax.nn.silu(causal_depthwise_conv(xbc, conv_w, conv_b))
    xs, bm, cm = jnp.split(xbc, [SSD_D_INNER, SSD_D_INNER + g * n], axis=-1)
    xs = xs.reshape(bsz, s, g, hg, p).astype(jnp.float32)
    bm = bm.reshape(bsz, s, g, n).astype(jnp.float32)
    cm = cm.reshape(bsz, s, g, n).astype(jnp.float32)
    dt = jax.nn.softplus((dt + dt_bias).astype(jnp.float32)).reshape(bsz, s, g, hg)
    a = -jnp.exp(a_log.astype(jnp.float32)).reshape(g, hg)
    y = ssd_chunked_scan(xs, dt, a, bm, cm)
    y = y + xs * d_skip.astype(jnp.float32).reshape(g, hg, 1)
    y = y.astype(h.dtype).reshape(bsz, s, SSD_D_INNER) * jax.nn.silu(z)
    y = rmsnorm(y.reshape(bsz, s, g, SSD_NORM_GROUP), norm_g.reshape(g, SSD_NORM_GROUP))
    return y.reshape(bsz, s, SSD_D_INNER) @ w_out


def hgrn2_chunked(q, k, v, log_f):
    bsz, _, hh, kd = q.shape
    vd = v.shape[-1]
    mask = jnp.tril(jnp.ones((HGRN_CHUNK, HGRN_CHUNK), bool))[None, :, :, None, None]

    def step(state, inp):
        qc, kc, vc, gc = inp
        bcum = jnp.cumsum(gc, axis=1)
        seg = bcum[:, :, None] - bcum[:, None, :]
        decay = jnp.exp(jnp.where(mask, seg, -jnp.inf))
        att = jnp.einsum('blhk,bshk,blshk->blsh', qc, kc, decay)
        o = jnp.einsum('blsh,bshv->blhv', att, vc)
        o = o + jnp.einsum('blhk,bhkv->blhv', qc * jnp.exp(bcum), state)
        tail = kc * jnp.exp(bcum[:, -1:] - bcum)
        state = (state * jnp.exp(bcum[:, -1])[..., None]
                 + jnp.einsum('bshk,bshv->bhkv', tail, vc))
        return state, o

    state0 = jnp.zeros((bsz, hh, kd, vd), jnp.float32)
    _, os_ = lax.scan(step, state0, (to_chunks(q, HGRN_CHUNK), to_chunks(k, HGRN_CHUNK),
                                     to_chunks(v, HGRN_CHUNK), to_chunks(log_f, HGRN_CHUNK)))
    return from_chunks(os_)


def hgrn_lower_bound(lb_logits, layer):
    p = jax.nn.softmax(lb_logits.astype(jnp.float32), axis=0)
    return jnp.cumsum(p, axis=0)[layer] - p[0]


def hgrn2_mixer(h, w_in, lower_bound, norm_g, w_out):
    bsz, s, _ = h.shape
    hh, kd, vd = HGRN_N_HEADS, HGRN_K_DIM, HGRN_V_DIM
    q, f, i, g = jnp.split(h @ w_in, [HGRN_FORGET_DIM, 2 * HGRN_FORGET_DIM,
                                      2 * HGRN_FORGET_DIM + D_MODEL], axis=-1)
    q = q.reshape(bsz, s, hh, kd).astype(jnp.float32)
    fl = f.reshape(bsz, s, hh, kd).astype(jnp.float32)
    i = i.reshape(bsz, s, hh, vd).astype(jnp.float32)
    lb = lower_bound.reshape(hh, kd)
    forget = lb + (1.0 - lb) * jax.nn.sigmoid(fl)
    log_f = jnp.log(forget)
    k = (1.0 - lb) * jax.nn.sigmoid(-fl)
    o = hgrn2_chunked(q, k, i, log_f).astype(h.dtype)
    o = rmsnorm(o, norm_g) * jax.nn.silu(g.reshape(bsz, s, hh, vd))
    return o.reshape(bsz, s, D_MODEL) @ w_out


def peer_ffn(h, w_query, sub_keys, expert_u, expert_v):
    bsz, s, d = h.shape
    nb = PEER_TOKEN_BLOCK
    blocks = h.reshape((bsz * s) // nb, nb, d)

    def one_block(xb):
        q = (xb @ w_query).reshape(nb, PEER_N_HEADS, 2, PEER_D_HALF)
        sc = jnp.einsum('nhcd,hckd->nhck', q, sub_keys).astype(jnp.float32)
        top_s, top_i = lax.top_k(sc, PEER_TOPK)
        cand_s = (top_s[:, :, 0, :, None] + top_s[:, :, 1, None, :]).reshape(nb, PEER_N_HEADS, -1)
        cand_i = (top_i[:, :, 0, :, None] * PEER_N_KEYS + top_i[:, :, 1, None, :]).reshape(nb, PEER_N_HEADS, -1)
        best_s, best_pos = lax.top_k(cand_s, PEER_TOPK)
        idx = jnp.take_along_axis(cand_i, best_pos, axis=-1)
        gate = jax.nn.softmax(best_s, axis=-1).astype(xb.dtype)
        u = expert_u[idx]
        v = expert_v[idx]
        act = jax.nn.gelu(jnp.einsum('nhkd,nd->nhk', u, xb), approximate=False)
        return jnp.einsum('nhk,nhkd->nd', gate * act, v)

    return lax.map(one_block, blocks).reshape(bsz, s, d)


def setup_inputs(seed: int = 0) -> dict:
    key = jax.random.key(seed)
    ks = jax.random.split(key, 24)
    f32 = jnp.float32
    nrm = lambda k, shape, scale: jax.random.normal(k, shape, f32) * scale
    x = jax.random.normal(ks[0], (BATCH, SEQ, D_MODEL), f32)
    mix_norm_g = 1.0 + nrm(ks[1], (DEPTH, D_MODEL), 0.02)
    ffn_norm_g = 1.0 + nrm(ks[2], (DEPTH, D_MODEL), 0.02)
    final_norm_g = 1.0 + nrm(ks[3], (D_MODEL,), 0.02)
    ssd_w_in = nrm(ks[4], (N_SSD_LAYERS, D_MODEL, SSD_IN_DIM), D_MODEL ** -0.5)
    ssd_conv_w = nrm(ks[5], (N_SSD_LAYERS, SSD_CONV_WIDTH, SSD_CONV_DIM), SSD_CONV_WIDTH ** -0.5)
    ssd_conv_b = nrm(ks[6], (N_SSD_LAYERS, SSD_CONV_DIM), 0.02)
    dt0 = jnp.exp(jax.random.uniform(ks[7], (N_SSD_LAYERS, SSD_N_HEADS), f32,
                                     jnp.log(1e-3), jnp.log(1e-1)))
    ssd_dt_bias = dt0 + jnp.log(-jnp.expm1(-dt0))
    ssd_a_log = jnp.log(jax.random.uniform(ks[8], (N_SSD_LAYERS, SSD_N_HEADS), f32, 1.0, 16.0))
    ssd_d_skip = 1.0 + nrm(ks[9], (N_SSD_LAYERS, SSD_N_HEADS), 0.02)
    ssd_norm_g = 1.0 + nrm(ks[10], (N_SSD_LAYERS, SSD_D_INNER), 0.02)
    ssd_w_out = nrm(ks[11], (N_SSD_LAYERS, SSD_D_INNER, D_MODEL), SSD_D_INNER ** -0.5)
    hgrn_w_in = nrm(ks[12], (N_HGRN_LAYERS, D_MODEL, HGRN_IN_DIM), D_MODEL ** -0.5)
    hgrn_lb_logits = 1.0 + nrm(ks[13], (DEPTH, HGRN_FORGET_DIM), 0.1)
    hgrn_norm_g = 1.0 + nrm(ks[14], (N_HGRN_LAYERS, HGRN_V_DIM), 0.02)
    hgrn_w_out = nrm(ks[15], (N_HGRN_LAYERS, D_MODEL, D_MODEL), D_MODEL ** -0.5)
    peer_w_query = nrm(ks[16], (DEPTH, D_MODEL, PEER_N_HEADS * PEER_D_QUERY), D_MODEL ** -0.5)
    peer_sub_keys = nrm(ks[17], (DEPTH, PEER_N_HEADS, 2, PEER_N_KEYS, PEER_D_HALF), PEER_D_HALF ** -0.5)
    peer_u = nrm(ks[18], (DEPTH, PEER_N_EXPERTS, D_MODEL), D_MODEL ** -0.5)
    peer_v = nrm(ks[19], (DEPTH, PEER_N_EXPERTS, D_MODEL), PEER_N_HEADS ** -0.5)
    return {'x': x, 'mix_norm_g': mix_norm_g, 'ffn_norm_g': ffn_norm_g,
            'final_norm_g': final_norm_g, 'ssd_w_in': ssd_w_in, 'ssd_conv_w': ssd_conv_w,
            'ssd_conv_b': ssd_conv_b, 'ssd_dt_bias': ssd_dt_bias, 'ssd_a_log': ssd_a_log,
            'ssd_d_skip': ssd_d_skip, 'ssd_norm_g': ssd_norm_g, 'ssd_w_out': ssd_w_out,
            'hgrn_w_in': hgrn_w_in, 'hgrn_lb_logits': hgrn_lb_logits,
            'hgrn_norm_g': hgrn_norm_g, 'hgrn_w_out': hgrn_w_out,
            'peer_w_query': peer_w_query, 'peer_sub_keys': peer_sub_keys,
            'peer_u': peer_u, 'peer_v': peer_v}


def reference(x, mix_norm_g, ffn_norm_g, final_norm_g, ssd_w_in, ssd_conv_w, ssd_conv_b,
              ssd_dt_bias, ssd_a_log, ssd_d_skip, ssd_norm_g, ssd_w_out, hgrn_w_in,
              hgrn_lb_logits, hgrn_norm_g, hgrn_w_out, peer_w_query, peer_sub_keys,
              peer_u, peer_v):
    for layer in range(DEPTH):
        h = rmsnorm(x, mix_norm_g[layer])
        j = layer // N_MIXERS
        if layer % N_MIXERS == 0:
            x = x + ssd_mixer(h, ssd_w_in[j], ssd_conv_w[j], ssd_conv_b[j], ssd_dt_bias[j],
                              ssd_a_log[j], ssd_d_skip[j], ssd_norm_g[j], ssd_w_out[j])
        else:
            lb = hgrn_lower_bound(hgrn_lb_logits, layer)
            x = x + hgrn2_mixer(h, hgrn_w_in[j], lb, hgrn_norm_g[j], hgrn_w_out[j])
        h = rmsnorm(x, ffn_norm_g[layer])
        x = x + peer_ffn(h, peer_w_query[layer], peer_sub_keys[layer], peer_u[layer], peer_v[layer])
    return rmsnorm(x, final_norm_g)
```

```python
import functools

import jax
import jax.numpy as jnp
from jax import lax
from jax.experimental import pallas as pl
from jax.experimental.pallas import tpu as pltpu

F32 = jnp.float32
BF16 = jnp.bfloat16
HIGHEST = lax.Precision.HIGHEST

LANES = 128
SUBLANES = 8
VMEM_LIMIT = 56 * 1024 * 1024

D_MODEL = 1024
NORM_EPS = 1e-6

SSD_D_INNER = 2048
SSD_HEAD_DIM = 64
SSD_N_HEADS = 32
SSD_N_GROUPS = 4
SSD_HEADS_PER_GROUP = 8
SSD_D_STATE = 128
SSD_CONV_WIDTH = 4
SSD_CONV_DIM = SSD_D_INNER + 2 * SSD_N_GROUPS * SSD_D_STATE
SSD_GROUP_DIM = SSD_D_INNER // SSD_N_GROUPS
CHUNK = 128

HGRN_N_HEADS = 8
HGRN_HEAD_DIM = 128

PEER_N_KEYS = 128
PEER_N_HEADS = 8
PEER_D_HALF = 128
PEER_TOPK = 16
PEER_SLOTS = PEER_N_HEADS * PEER_TOPK
ROW_WORDS = D_MODEL // 2
ROW_SUB = ROW_WORDS // LANES


def _rmsnorm(x, g):
    r = lax.rsqrt(jnp.mean(x * x, axis=-1, keepdims=True) + NORM_EPS)
    return (x * r) * g


def _silu(x):
    return x * (1.0 / (1.0 + jnp.exp(-x)))


def _sigmoid(x):
    return 1.0 / (1.0 + jnp.exp(-x))


def _params(*sem):
    return pltpu.CompilerParams(dimension_semantics=sem, vmem_limit_bytes=VMEM_LIMIT)


def _norm_matmul_kernel(x_ref, g_ref, w_ref, o_ref, h_scr):
    @pl.when(pl.program_id(1) == 0)
    def _():
        h_scr[...] = _rmsnorm(x_ref[...], g_ref[...]).astype(BF16)

    o_ref[...] = jnp.dot(h_scr[...], w_ref[...], preferred_element_type=F32).astype(o_ref.dtype)


def _norm_matmul(x, g, w, tm=512, tn=512, out_dtype=BF16):
    t, d = x.shape
    n = w.shape[1]
    return pl.pallas_call(
        _norm_matmul_kernel,
        grid=(t // tm, n // tn),
        in_specs=[pl.BlockSpec((tm, d), lambda i, j: (i, 0)),
                  pl.BlockSpec((1, d), lambda i, j: (0, 0)),
                  pl.BlockSpec((d, tn), lambda i, j: (0, j))],
        out_specs=pl.BlockSpec((tm, tn), lambda i, j: (i, j)),
        out_shape=jax.ShapeDtypeStruct((t, n), out_dtype),
        scratch_shapes=[pltpu.VMEM((tm, d), BF16)],
        compiler_params=_params("parallel", "arbitrary"),
        name="norm_matmul",
    )(x, g.reshape(1, d), w)


def _matmul_residual_kernel(x_ref, a_ref, w_ref, o_ref):
    o_ref[...] = x_ref[...] + jnp.dot(a_ref[...], w_ref[...], preferred_element_type=F32)


def _matmul_residual(x, a, w, tm=512):
    t, d = x.shape
    k = a.shape[1]
    return pl.pallas_call(
        _matmul_residual_kernel,
        grid=(t // tm,),
        in_specs=[pl.BlockSpec((tm, d), lambda i: (i, 0)),
                  pl.BlockSpec((tm, k), lambda i: (i, 0)),
                  pl.BlockSpec((k, d), lambda i: (0, 0))],
        out_specs=pl.BlockSpec((tm, d), lambda i: (i, 0)),
        out_shape=jax.ShapeDtypeStruct((t, d), F32),
        compiler_params=_params("parallel"),
        name="matmul_residual",
    )(x, a, w)


def _final_norm_kernel(x_ref, g_ref, o_ref):
    o_ref[...] = _rmsnorm(x_ref[...], g_ref[...])


def _final_norm(x, g, tm=1024):
    t, d = x.shape
    return pl.pallas_call(
        _final_norm_kernel,
        grid=(t // tm,),
        in_specs=[pl.BlockSpec((tm, d), lambda i: (i, 0)),
                  pl.BlockSpec((1, d), lambda i: (0, 0))],
        out_specs=pl.BlockSpec((tm, d), lambda i: (i, 0)),
        out_shape=jax.ShapeDtypeStruct((t, d), F32),
        compiler_params=_params("parallel"),
        name="final_norm",
    )(x, g.reshape(1, d))


def _ssd_kernel(x_ref, zx_ref, gmix_ref, wdt_ref, dtb_ref, alog_ref, convw_ref, convb_ref,
                dskip_ref, expand_ref, gnorm_ref, wout_ref, o_ref, state_scr, tail_scr):
    L = CHUNK
    G, N, P = SSD_N_GROUPS, SSD_D_STATE, SSD_HEAD_DIM
    GD = SSD_GROUP_DIM

    @pl.when(pl.program_id(1) == 0)
    def _():
        state_scr[...] = jnp.zeros_like(state_scr)
        tail_scr[...] = jnp.zeros_like(tail_scr)

    x = x_ref[...]
    h = _rmsnorm(x, gmix_ref[...]).astype(BF16)
    dt_raw = jnp.dot(h, wdt_ref[...], preferred_element_type=F32) + dtb_ref[...]
    dt = jnp.maximum(dt_raw, 0.0) + jnp.log(1.0 + jnp.exp(-jnp.abs(dt_raw)))
    a = -jnp.exp(alog_ref[...])
    row = lax.broadcasted_iota(jnp.int32, (L, L), 0)
    col = lax.broadcasted_iota(jnp.int32, (L, L), 1)
    causal = row >= col
    tril = causal.astype(F32)
    acs = jnp.dot(tril, dt * a, preferred_element_type=F32, precision=HIGHEST)
    acs_t = acs.T
    dt_t = dt.T
    exp_acs = jnp.exp(acs)
    tail = jnp.exp(acs[L - 1:L, :] - acs) * dt
    both = jnp.concatenate([exp_acs, tail], axis=0)
    both_e = jnp.dot(both, expand_ref[...], preferred_element_type=F32, precision=HIGHEST)
    exp_acs_e = both_e[:L]
    tail_e = both_e[L:]

    zx = zx_ref[...]
    z = zx[:, :SSD_D_INNER].astype(F32)
    xbc = zx[:, SSD_D_INNER:].astype(F32)
    ext = jnp.concatenate([tail_scr[...], xbc], axis=0)
    tail_scr[...] = xbc[L - SUBLANES:, :]
    conv = convb_ref[...] + convw_ref[SSD_CONV_WIDTH - 1:SSD_CONV_WIDTH, :] * xbc
    for k in range(SSD_CONV_WIDTH - 1):
        off = SUBLANES - (SSD_CONV_WIDTH - 1) + k
        conv = conv + convw_ref[k:k + 1, :] * ext[off:off + L, :]
    xbc = _silu(conv)
    xs = xbc[:, :SSD_D_INNER]
    bm = xbc[:, SSD_D_INNER:SSD_D_INNER + G * N]
    cm = xbc[:, SSD_D_INNER + G * N:]

    xs_tail = (xs * tail_e).astype(BF16)
    xs_b = xs.astype(BF16)
    y_parts = []
    for g in range(G):
        b_g = bm[:, g * N:(g + 1) * N].astype(BF16)
        c_g = cm[:, g * N:(g + 1) * N].astype(BF16)
        cb = lax.dot_general(c_g, b_g, (((1,), (1,)), ((), ())), preferred_element_type=F32)
        st = state_scr[g]
        y_inter = jnp.dot(c_g, st.astype(BF16), preferred_element_type=F32)
        y_g = y_inter * exp_acs_e[:, g * GD:(g + 1) * GD]
        intra = []
        for j in range(SSD_HEADS_PER_GROUP):
            hh = g * SSD_HEADS_PER_GROUP + j
            seg = acs[:, hh:hh + 1] - acs_t[hh:hh + 1, :]
            decay = jnp.exp(jnp.where(causal, seg, -jnp.inf))
            w = (cb * decay * dt_t[hh:hh + 1, :]).astype(BF16)
            intra.append(jnp.dot(w, xs_b[:, hh * P:(hh + 1) * P], preferred_element_type=F32))
        y_g = y_g + jnp.concatenate(intra, axis=1)
        y_parts.append(y_g)
        upd = lax.dot_general(b_g, xs_tail[:, g * GD:(g + 1) * GD], (((0,), (0,)), ((), ())),
                              preferred_element_type=F32)
        state_scr[g] = st * exp_acs_e[L - 1:L, g * GD:(g + 1) * GD] + upd
    y = jnp.concatenate(y_parts, axis=1)
    y = y + xs * dskip_ref[...]
    y = y * _silu(z)
    normed = []
    for g in range(G):
        y_g = y[:, g * GD:(g + 1) * GD]
        normed.append(_rmsnorm(y_g, gnorm_ref[:, g * GD:(g + 1) * GD]))
    yn = jnp.concatenate(normed, axis=1).astype(BF16)
    o_ref[...] = x + jnp.dot(yn, wout_ref[...], preferred_element_type=F32)


def _pad_lanes(v):
    return jnp.pad(v.astype(F32), (0, LANES - v.shape[0])).reshape(1, LANES)


def _ssd_layer(x, bsz, g_mix, w_in, conv_w, conv_b, dt_bias, a_log, d_skip, norm_g, w_out):
    t, d = x.shape
    nc = t // bsz // CHUNK
    w_zx = w_in[:, :SSD_D_INNER + SSD_CONV_DIM].astype(BF16)
    w_dt = jnp.pad(w_in[:, SSD_D_INNER + SSD_CONV_DIM:], ((0, 0), (0, LANES - SSD_N_HEADS))).astype(BF16)
    zx = _norm_matmul(x, g_mix, w_zx)
    head_of_lane = jnp.arange(SSD_D_INNER) // SSD_HEAD_DIM
    expand = (jnp.arange(LANES)[:, None] == head_of_lane[None, :]).astype(F32)
    dskip_e = jnp.repeat(d_skip.astype(F32), SSD_HEAD_DIM).reshape(1, SSD_D_INNER)
    const = lambda shape: pl.BlockSpec(shape, lambda b, c: (0,) * len(shape))
    tok = lambda width: pl.BlockSpec((CHUNK, width), lambda b, c: (b * nc + c, 0))
    return pl.pallas_call(
        _ssd_kernel,
        grid=(bsz, nc),
        in_specs=[tok(d), tok(SSD_D_INNER + SSD_CONV_DIM), const((1, d)), const((d, LANES)),
                  const((1, LANES)), const((1, LANES)), const((SSD_CONV_WIDTH, SSD_CONV_DIM)),
                  const((1, SSD_CONV_DIM)), const((1, SSD_D_INNER)), const((LANES, SSD_D_INNER)),
                  const((1, SSD_D_INNER)), const((SSD_D_INNER, d))],
        out_specs=tok(d),
        out_shape=jax.ShapeDtypeStruct((t, d), F32),
        scratch_shapes=[pltpu.VMEM((SSD_N_GROUPS, SSD_D_STATE, SSD_GROUP_DIM), F32),
                        pltpu.VMEM((SUBLANES, SSD_CONV_DIM), F32)],
        compiler_params=_params("parallel", "arbitrary"),
        name="ssd_scan",
    )(x, zx, g_mix.reshape(1, d), w_dt, _pad_lanes(dt_bias), _pad_lanes(a_log), conv_w.astype(F32),
      conv_b.reshape(1, -1).astype(F32), dskip_e, expand, norm_g.reshape(1, -1).astype(F32),
      w_out.astype(BF16))


def _hgrn_kernel(q_ref, f_ref, i_ref, g_ref, lb_ref, gnorm_ref, o_ref, state_scr, f_scr):
    L = CHUNK

    @pl.when(pl.program_id(2) == 0)
    def _():
        state_scr[...] = jnp.zeros_like(state_scr)

    lb = lb_ref[...]
    fl = f_ref[...].astype(F32)
    f_scr[...] = lb + (1.0 - lb) * _sigmoid(fl)
    kk = ((1.0 - lb) * _sigmoid(-fl)).astype(BF16)
    q = q_ref[...]
    v_t = i_ref[...].astype(F32).T
    lane = lax.broadcasted_iota(jnp.int32, (HGRN_HEAD_DIM, L), 1)

    def token(t, carry):
        s, o_t = carry
        here = lane == t
        outer = jnp.dot(jnp.where(here, v_t, 0.0).astype(BF16), kk, preferred_element_type=F32)
        s = s * f_scr[pl.ds(t, 1), :] + outer
        r = lax.dot_general(s.astype(BF16), q, (((1,), (1,)), ((), ())), preferred_element_type=F32)
        return s, jnp.where(here, r, o_t)

    s, o_t = lax.fori_loop(0, L, token, (state_scr[...], jnp.zeros((HGRN_HEAD_DIM, L), F32)))
    state_scr[...] = s
    o = _rmsnorm(o_t.T, gnorm_ref[...]) * _silu(g_ref[...].astype(F32))
    o_ref[...] = o.astype(o_ref.dtype)


def _hgrn_layer(x, bsz, g_mix, w_in, lower_bound, norm_g, w_out):
    t, d = x.shape
    nc = t // bsz // CHUNK
    hd = HGRN_HEAD_DIM
    qfig = _norm_matmul(x, g_mix, w_in.astype(BF16))
    part = lambda k: pl.BlockSpec((CHUNK, hd), lambda b, h, c: (b * nc + c, k * HGRN_N_HEADS + h))
    o = pl.pallas_call(
        _hgrn_kernel,
        grid=(bsz, HGRN_N_HEADS, nc),
        in_specs=[part(0), part(1), part(2), part(3),
                  pl.BlockSpec((1, hd), lambda b, h, c: (0, h)),
                  pl.BlockSpec((1, hd), lambda b, h, c: (0, 0))],
        out_specs=pl.BlockSpec((CHUNK, hd), lambda b, h, c: (b * nc + c, h)),
        out_shape=jax.ShapeDtypeStruct((t, d), BF16),
        scratch_shapes=[pltpu.VMEM((hd, hd), F32), pltpu.VMEM((CHUNK, hd), F32)],
        compiler_params=_params("parallel", "parallel", "arbitrary"),
        name="hgrn_scan",
    )(qfig, qfig, qfig, qfig, lower_bound.reshape(1, d).astype(F32), norm_g.reshape(1, hd).astype(F32))
    return _matmul_residual(x, o, w_out.astype(BF16))


def _topk_rows(s, k, payload=None):
    r = s.shape[0]
    rows = lax.broadcasted_iota(jnp.int32, s.shape, 0)
    vals, picks = [], []
    for _ in range(k):
        m = jnp.max(s, axis=0, keepdims=True)
        pos = jnp.min(jnp.where(s == m, rows, r), axis=0, keepdims=True)
        hit = rows == pos
        vals.append(m)
        picks.append(pos if payload is None else jnp.sum(jnp.where(hit, payload, 0), axis=0, keepdims=True))
        s = jnp.where(hit, -jnp.inf, s)
    return jnp.concatenate(vals, axis=0), jnp.concatenate(picks, axis=0)


def _peer_route_kernel(x_ref, g_ref, wq_ref, keys_ref, h_ref, idx_ref, gate_ref):
    h = _rmsnorm(x_ref[...], g_ref[...])
    h_ref[...] = h
    q_t = lax.dot_general(wq_ref[...], h.astype(BF16), (((1,), (1,)), ((), ())),
                          preferred_element_type=F32)
    idx_rows, gate_rows = [], []
    for head in range(PEER_N_HEADS):
        top_s, top_i = [], []
        for c in range(2):
            hc = head * 2 + c
            q_hc = q_t[hc * PEER_D_HALF:(hc + 1) * PEER_D_HALF, :].astype(BF16)
            sc = jnp.dot(keys_ref[hc], q_hc, preferred_element_type=F32)
            s, i = _topk_rows(sc, PEER_TOPK)
            top_s.append(s)
            top_i.append(i)
        cand_s = jnp.concatenate([top_s[0][a:a + 1, :] + top_s[1] for a in range(PEER_TOPK)], axis=0)
        cand_i = jnp.concatenate([top_i[0][a:a + 1, :] * PEER_N_KEYS + top_i[1] for a in range(PEER_TOPK)],
                                 axis=0)
        best_s, best_i = _topk_rows(cand_s, PEER_TOPK, payload=cand_i)
        e = jnp.exp(best_s - best_s[0:1, :])
        gate_rows.append(e / jnp.sum(e, axis=0, keepdims=True))
        idx_rows.append(best_i)
    idx_ref[...] = jnp.concatenate(idx_rows, axis=0).astype(F32).T.astype(jnp.int32)
    gate_ref[...] = jnp.concatenate(gate_rows, axis=0).T


def _peer_route(x, g, w_query, sub_keys, tn=128):
    t, d = x.shape
    nq = w_query.shape[1]
    keys = sub_keys.reshape(2 * PEER_N_HEADS, PEER_N_KEYS, PEER_D_HALF).astype(BF16)
    return pl.pallas_call(
        _peer_route_kernel,
        grid=(t // tn,),
        in_specs=[pl.BlockSpec((tn, d), lambda i: (i, 0)),
                  pl.BlockSpec((1, d), lambda i: (0, 0)),
                  pl.BlockSpec((nq, d), lambda i: (0, 0)),
                  pl.BlockSpec(keys.shape, lambda i: (0, 0, 0))],
        out_specs=[pl.BlockSpec((tn, d), lambda i: (i, 0)),
                   pl.BlockSpec((tn, PEER_SLOTS), lambda i: (i, 0)),
                   pl.BlockSpec((tn, PEER_SLOTS), lambda i: (i, 0))],
        out_shape=[jax.ShapeDtypeStruct((t, d), F32),
                   jax.ShapeDtypeStruct((t, PEER_SLOTS), jnp.int32),
                   jax.ShapeDtypeStruct((t, PEER_SLOTS), F32)],
        compiler_params=_params("parallel"),
        name="peer_route",
    )(x, g.reshape(1, d), w_query.T.astype(BF16), keys)


def _pack_table(t):
    e = t.shape[0]
    b = lax.bitcast_convert_type(t.astype(BF16), jnp.uint16).astype(jnp.uint32)
    w = b[:, :ROW_WORDS] | (b[:, ROW_WORDS:] << 16)
    return w.reshape(e, ROW_SUB, LANES)


def _unpack_row(w):
    lo = lax.bitcast_convert_type(w << 16, F32)
    hi = lax.bitcast_convert_type(w & jnp.uint32(0xFFFF0000), F32)
    return lo, hi


def _peer_u_kernel(idx_ref, h_ref, gate_ref, tab_ref, c_ref, pbuf, act_buf):
    tb = h_ref.shape[0]

    def token(t, carry):
        xt = h_ref[t]
        x_lo = xt[0:ROW_SUB]
        x_hi = xt[ROW_SUB:2 * ROW_SUB]
        for p in range(PEER_SLOTS):
            lo, hi = _unpack_row(tab_ref[idx_ref[t, p]])
            pbuf[p * ROW_SUB:(p + 1) * ROW_SUB, :] = lo * x_lo + hi * x_hi
        q = pbuf[pl.ds(0, PEER_SLOTS, stride=ROW_SUB), :]
        for s in range(1, ROW_SUB):
            q = q + pbuf[pl.ds(s, PEER_SLOTS, stride=ROW_SUB), :]
        act_buf[pl.ds(t, 1), :] = jnp.sum(q.T, axis=0, keepdims=True)
        return carry

    lax.fori_loop(0, tb, token, 0)
    act = act_buf[...]
    gelu = 0.5 * act * (1.0 + lax.erf(act * (2.0 ** -0.5)))
    c_ref[...] = gate_ref[...] * gelu


def _peer_v_kernel(idx_ref, c_ref, x_ref, tab_ref, o_ref):
    tb = x_ref.shape[0]
    n_acc = 4

    def token(t, carry):
        acc_lo = [jnp.zeros((ROW_SUB, LANES), F32) for _ in range(n_acc)]
        acc_hi = [jnp.zeros((ROW_SUB, LANES), F32) for _ in range(n_acc)]
        for p in range(PEER_SLOTS):
            lo, hi = _unpack_row(tab_ref[idx_ref[t, p]])
            c = c_ref[t, p]
            acc_lo[p % n_acc] = acc_lo[p % n_acc] + c * lo
            acc_hi[p % n_acc] = acc_hi[p % n_acc] + c * hi
        lo = (acc_lo[0] + acc_lo[1]) + (acc_lo[2] + acc_lo[3])
        hi = (acc_hi[0] + acc_hi[1]) + (acc_hi[2] + acc_hi[3])
        o_ref[t] = x_ref[t] + jnp.concatenate([lo, hi], axis=0)
        return carry

    lax.fori_loop(0, tb, token, 0)


def _peer_gather(x3, h3, idx, gate, u_packed, v_packed, tb=128):
    t = x3.shape[0]
    e = u_packed.shape[0]
    grid = (t // tb,)
    tok3 = pl.BlockSpec((tb, SUBLANES, LANES), lambda i: (i, 0, 0))
    tok2 = pl.BlockSpec((tb, PEER_SLOTS), lambda i: (i, 0))
    smem2 = pl.BlockSpec((tb, PEER_SLOTS), lambda i: (i, 0), memory_space=pltpu.SMEM)
    table = pl.BlockSpec((e, ROW_SUB, LANES), lambda i: (0, 0, 0), pipeline_mode=pl.Buffered(1))
    c = pl.pallas_call(
        _peer_u_kernel,
        grid=grid,
        in_specs=[smem2, tok3, tok2, table],
        out_specs=tok2,
        out_shape=jax.ShapeDtypeStruct((t, PEER_SLOTS), F32),
        scratch_shapes=[pltpu.VMEM((PEER_SLOTS * ROW_SUB, LANES), F32),
                        pltpu.VMEM((tb, PEER_SLOTS), F32)],
        compiler_params=_params("arbitrary"),
        name="peer_u",
    )(idx, h3, gate, u_packed)
    return pl.pallas_call(
        _peer_v_kernel,
        grid=grid,
        in_specs=[smem2, smem2, tok3, table],
        out_specs=tok3,
        out_shape=jax.ShapeDtypeStruct((t, SUBLANES, LANES), F32),
        compiler_params=_params("arbitrary"),
        name="peer_v",
    )(idx, c, x3, v_packed)


def _peer_layer(x, g, w_query, sub_keys, expert_u, expert_v):
    t, d = x.shape
    h, idx, gate = _peer_route(x, g, w_query, sub_keys)
    x3 = x.reshape(t, SUBLANES, LANES)
    h3 = h.reshape(t, SUBLANES, LANES)
    y3 = _peer_gather(x3, h3, idx, gate, _pack_table(expert_u), _pack_table(expert_v))
    return y3.reshape(t, d)


def _hgrn_lower_bound(lb_logits, layer):
    p = jax.nn.softmax(lb_logits.astype(F32), axis=0)
    return jnp.cumsum(p, axis=0)[layer] - p[0]


def kernel(x, mix_norm_g, ffn_norm_g, final_norm_g, ssd_w_in, ssd_conv_w, ssd_conv_b, ssd_dt_bias, ssd_a_log, ssd_d_skip, ssd_norm_g, ssd_w_out, hgrn_w_in, hgrn_lb_logits, hgrn_norm_g, hgrn_w_out, peer_w_query, peer_sub_keys, peer_u, peer_v):
    bsz, s, d = x.shape
    depth = mix_norm_g.shape[0]
    xf = x.reshape(bsz * s, d)
    for layer in range(depth):
        j = layer // 2
        if layer % 2 == 0:
            xf = _ssd_layer(xf, bsz, mix_norm_g[layer], ssd_w_in[j], ssd_conv_w[j], ssd_conv_b[j],
                            ssd_dt_bias[j], ssd_a_log[j], ssd_d_skip[j], ssd_norm_g[j], ssd_w_out[j])
        else:
            lb = _hgrn_lower_bound(hgrn_lb_logits, layer)
            xf = _hgrn_layer(xf, bsz, mix_norm_g[layer], hgrn_w_in[j], lb, hgrn_norm_g[j], hgrn_w_out[j])
        xf = _peer_layer(xf, ffn_norm_g[layer], peer_w_query[layer], peer_sub_keys[layer],
                         peer_u[layer], peer_v[layer])
    return _final_norm(xf, final_norm_g).reshape(bsz, s, d)
```

```python
import functools

import jax
import jax.numpy as jnp
from jax import lax
from jax.experimental import pallas as pl
from jax.experimental.pallas import tpu as pltpu

F32 = jnp.float32
BF16 = jnp.bfloat16
HIGHEST = lax.Precision.HIGHEST

LANES = 128
SUBLANES = 8
VMEM_LIMIT = 56 * 1024 * 1024

D_MODEL = 1024
NORM_EPS = 1e-6

SSD_D_INNER = 2048
SSD_HEAD_DIM = 64
SSD_N_HEADS = 32
SSD_N_GROUPS = 4
SSD_HEADS_PER_GROUP = 8
SSD_D_STATE = 128
SSD_CONV_WIDTH = 4
SSD_CONV_DIM = SSD_D_INNER + 2 * SSD_N_GROUPS * SSD_D_STATE
SSD_GROUP_DIM = SSD_D_INNER // SSD_N_GROUPS
CHUNK = 128

HGRN_N_HEADS = 8
HGRN_HEAD_DIM = 128

PEER_N_KEYS = 128
PEER_N_HEADS = 8
PEER_D_HALF = 128
PEER_TOPK = 16
PEER_SLOTS = PEER_N_HEADS * PEER_TOPK
ROW_WORDS = D_MODEL // 2
ROW_SUB = ROW_WORDS // LANES


def _rmsnorm(x, g):
    r = lax.rsqrt(jnp.mean(x * x, axis=-1, keepdims=True) + NORM_EPS)
    return (x * r) * g


def _silu(x):
    return x * (1.0 / (1.0 + jnp.exp(-x)))


def _sigmoid(x):
    return 1.0 / (1.0 + jnp.exp(-x))


def _params(*sem):
    return pltpu.CompilerParams(dimension_semantics=sem, vmem_limit_bytes=VMEM_LIMIT)


def _norm_matmul_kernel(x_ref, g_ref, w_ref, o_ref, h_scr):
    @pl.when(pl.program_id(1) == 0)
    def _():
        h_scr[...] = _rmsnorm(x_ref[...], g_ref[...]).astype(BF16)

    o_ref[...] = jnp.dot(h_scr[...], w_ref[...], preferred_element_type=F32).astype(o_ref.dtype)


def _norm_matmul(x, g, w, tm=512, tn=512, out_dtype=BF16):
    t, d = x.shape
    n = w.shape[1]
    return pl.pallas_call(
        _norm_matmul_kernel,
        grid=(t // tm, n // tn),
        in_specs=[pl.BlockSpec((tm, d), lambda i, j: (i, 0)),
                  pl.BlockSpec((1, d), lambda i, j: (0, 0)),
                  pl.BlockSpec((d, tn), lambda i, j: (0, j))],
        out_specs=pl.BlockSpec((tm, tn), lambda i, j: (i, j)),
        out_shape=jax.ShapeDtypeStruct((t, n), out_dtype),
        scratch_shapes=[pltpu.VMEM((tm, d), BF16)],
        compiler_params=_params("parallel", "arbitrary"),
        name="norm_matmul",
    )(x, g.reshape(1, d), w)


def _matmul_residual_kernel(x_ref, a_ref, w_ref, o_ref):
    o_ref[...] = x_ref[...] + jnp.dot(a_ref[...], w_ref[...], preferred_element_type=F32)


def _matmul_residual(x, a, w, tm=512):
    t, d = x.shape
    k = a.shape[1]
    return pl.pallas_call(
        _matmul_residual_kernel,
        grid=(t // tm,),
        in_specs=[pl.BlockSpec((tm, d), lambda i: (i, 0)),
                  pl.BlockSpec((tm, k), lambda i: (i, 0)),
                  pl.BlockSpec((k, d), lambda i: (0, 0))],
        out_specs=pl.BlockSpec((tm, d), lambda i: (i, 0)),
        out_shape=jax.ShapeDtypeStruct((t, d), F32),
        compiler_params=_params("parallel"),
        name="matmul_residual",
    )(x, a, w)


def _final_norm_kernel(x_ref, g_ref, o_ref):
    o_ref[...] = _rmsnorm(x_ref[...], g_ref[...])


def _final_norm(x, g, tm=1024):
    t, d = x.shape
    return pl.pallas_call(
        _final_norm_kernel,
        grid=(t // tm,),
        in_specs=[pl.BlockSpec((tm, d), lambda i: (i, 0)),
                  pl.BlockSpec((1, d), lambda i: (0, 0))],
        out_specs=pl.BlockSpec((tm, d), lambda i: (i, 0)),
        out_shape=jax.ShapeDtypeStruct((t, d), F32),
        compiler_params=_params("parallel"),
        name="final_norm",
    )(x, g.reshape(1, d))


def _ssd_kernel(x_ref, zx_ref, gmix_ref, wdt_ref, dtb_ref, alog_ref, convw_ref, convb_ref,
                dskip_ref, expand_ref, gnorm_ref, wout_ref, o_ref, state_scr, tail_scr):
    L = CHUNK
    G, N, P = SSD_N_GROUPS, SSD_D_STATE, SSD_HEAD_DIM
    GD = SSD_GROUP_DIM

    @pl.when(pl.program_id(1) == 0)
    def _():
        state_scr[...] = jnp.zeros_like(state_scr)
        tail_scr[...] = jnp.zeros_like(tail_scr)

    x = x_ref[...]
    h = _rmsnorm(x, gmix_ref[...]).astype(BF16)
    dt_raw = jnp.dot(h, wdt_ref[...], preferred_element_type=F32) + dtb_ref[...]
    dt = jnp.maximum(dt_raw, 0.0) + jnp.log(1.0 + jnp.exp(-jnp.abs(dt_raw)))
    a = -jnp.exp(alog_ref[...])
    row = lax.broadcasted_iota(jnp.int32, (L, L), 0)
    col = lax.broadcasted_iota(jnp.int32, (L, L), 1)
    causal = row >= col
    tril = causal.astype(F32)
    acs = jnp.dot(tril, dt * a, preferred_element_type=F32, precision=HIGHEST)
    acs_t = acs.T
    dt_t = dt.T
    exp_acs = jnp.exp(acs)
    tail = jnp.exp(acs[L - 1:L, :] - acs) * dt
    both = jnp.concatenate([exp_acs, tail], axis=0)
    both_e = jnp.dot(both, expand_ref[...], preferred_element_type=F32, precision=HIGHEST)
    exp_acs_e = both_e[:L]
    tail_e = both_e[L:]

    zx = zx_ref[...]
    z = zx[:, :SSD_D_INNER].astype(F32)
    xbc = zx[:, SSD_D_INNER:].astype(F32)
    ext = jnp.concatenate([tail_scr[...], xbc], axis=0)
    tail_scr[...] = xbc[L - SUBLANES:, :]
    conv = convb_ref[...] + convw_ref[SSD_CONV_WIDTH - 1:SSD_CONV_WIDTH, :] * xbc
    for k in range(SSD_CONV_WIDTH - 1):
        off = SUBLANES - (SSD_CONV_WIDTH - 1) + k
        conv = conv + convw_ref[k:k + 1, :] * ext[off:off + L, :]
    xbc = _silu(conv)
    xs = xbc[:, :SSD_D_INNER]
    bm = xbc[:, SSD_D_INNER:SSD_D_INNER + G * N]
    cm = xbc[:, SSD_D_INNER + G * N:]

    xs_tail = (xs * tail_e).astype(BF16)
    xs_b = xs.astype(BF16)
    y_parts = []
    for g in range(G):
        b_g = bm[:, g * N:(g + 1) * N].astype(BF16)
        c_g = cm[:, g * N:(g + 1) * N].astype(BF16)
        cb = lax.dot_general(c_g, b_g, (((1,), (1,)), ((), ())), preferred_element_type=F32)
        st = state_scr[g]
        y_inter = jnp.dot(c_g, st.astype(BF16), preferred_element_type=F32)
        y_g = y_inter * exp_acs_e[:, g * GD:(g + 1) * GD]
        intra = []
        for j in range(SSD_HEADS_PER_GROUP):
            hh = g * SSD_HEADS_PER_GROUP + j
            seg = acs[:, hh:hh + 1] - acs_t[hh:hh + 1, :]
            decay = jnp.exp(jnp.where(causal, seg, -jnp.inf))
            w = (cb * decay * dt_t[hh:hh + 1, :]).astype(BF16)
            intra.append(jnp.dot(w, xs_b[:, hh * P:(hh + 1) * P], preferred_element_type=F32))
        y_g = y_g + jnp.concatenate(intra, axis=1)
        y_parts.append(y_g)
        upd = lax.dot_general(b_g, xs_tail[:, g * GD:(g + 1) * GD], (((0,), (0,)), ((), ())),
                              preferred_element_type=F32)
        state_scr[g] = st * exp_acs_e[L - 1:L, g * GD:(g + 1) * GD] + upd
    y = jnp.concatenate(y_parts, axis=1)
    y = y + xs * dskip_ref[...]
    y = y * _silu(z)
    normed = []
    for g in range(G):
        y_g = y[:, g * GD:(g + 1) * GD]
        normed.append(_rmsnorm(y_g, gnorm_ref[:, g * GD:(g + 1) * GD]))
    yn = jnp.concatenate(normed, axis=1).astype(BF16)
    o_ref[...] = x + jnp.dot(yn, wout_ref[...], preferred_element_type=F32)


def _pad_lanes(v):
    return jnp.pad(v.astype(F32), (0, LANES - v.shape[0])).reshape(1, LANES)


def _ssd_layer(x, bsz, g_mix, w_in, conv_w, conv_b, dt_bias, a_log, d_skip, norm_g, w_out):
    t, d = x.shape
    nc = t // bsz // CHUNK
    w_zx = w_in[:, :SSD_D_INNER + SSD_CONV_DIM].astype(BF16)
    w_dt = jnp.pad(w_in[:, SSD_D_INNER + SSD_CONV_DIM:], ((0, 0), (0, LANES - SSD_N_HEADS))).astype(BF16)
    zx = _norm_matmul(x, g_mix, w_zx)
    head_of_lane = jnp.arange(SSD_D_INNER) // SSD_HEAD_DIM
    expand = (jnp.arange(LANES)[:, None] == head_of_lane[None, :]).astype(F32)
    dskip_e = jnp.repeat(d_skip.astype(F32), SSD_HEAD_DIM).reshape(1, SSD_D_INNER)
    const = lambda shape: pl.BlockSpec(shape, lambda b, c: (0,) * len(shape))
    tok = lambda width: pl.BlockSpec((CHUNK, width), lambda b, c: (b * nc + c, 0))
    return pl.pallas_call(
        _ssd_kernel,
        grid=(bsz, nc),
        in_specs=[tok(d), tok(SSD_D_INNER + SSD_CONV_DIM), const((1, d)), const((d, LANES)),
                  const((1, LANES)), const((1, LANES)), const((SSD_CONV_WIDTH, SSD_CONV_DIM)),
                  const((1, SSD_CONV_DIM)), const((1, SSD_D_INNER)), const((LANES, SSD_D_INNER)),
                  const((1, SSD_D_INNER)), const((SSD_D_INNER, d))],
        out_specs=tok(d),
        out_shape=jax.ShapeDtypeStruct((t, d), F32),
        scratch_shapes=[pltpu.VMEM((SSD_N_GROUPS, SSD_D_STATE, SSD_GROUP_DIM), F32),
                        pltpu.VMEM((SUBLANES, SSD_CONV_DIM), F32)],
        compiler_params=_params("parallel", "arbitrary"),
        name="ssd_scan",
    )(x, zx, g_mix.reshape(1, d), w_dt, _pad_lanes(dt_bias), _pad_lanes(a_log), conv_w.astype(F32),
      conv_b.reshape(1, -1).astype(F32), dskip_e, expand, norm_g.reshape(1, -1).astype(F32),
      w_out.astype(BF16))


HGRN_BLOCK = SUBLANES
HGRN_PAIRS = CHUNK * HGRN_BLOCK


def _split3(x):
    x1 = x.astype(BF16)
    r1 = x - x1.astype(F32)
    x2 = r1.astype(BF16)
    x3 = (r1 - x2.astype(F32)).astype(BF16)
    return x1, x2, x3


def _hgrn_kernel(q_ref, f_ref, i_ref, g_ref, lb_ref, gnorm_ref, rep_ref, tile_ref, dif_ref, fold_ref,
                 o_ref, state_scr):
    L, C = CHUNK, HGRN_BLOCK
    NB = L // C
    K = HGRN_HEAD_DIM
    nt = (((1,), (1,)), ((), ()))

    @pl.when(pl.program_id(2) == 0)
    def _():
        state_scr[...] = jnp.zeros_like(state_scr)

    lb = lb_ref[...]
    fl = f_ref[...].astype(F32)
    logf = jnp.log(lb + (1.0 - lb) * _sigmoid(fl))
    kk = (1.0 - lb) * _sigmoid(-fl)
    q = q_ref[...].astype(F32)
    row = lax.broadcasted_iota(jnp.int32, (L, L), 0)
    col = lax.broadcasted_iota(jnp.int32, (L, L), 1)
    b = jnp.dot((row >= col).astype(F32), logf, preferred_element_type=F32, precision=HIGHEST)
    b3 = b.reshape(NB, C, K)
    b_end3 = jnp.broadcast_to(b3[:, C - 1:C, :], b3.shape)
    b_prev3 = jnp.concatenate([jnp.zeros((1, C, K), F32), b_end3[:NB - 1]], axis=0)
    b_end = b_end3.reshape(L, K)
    b_prev = b_prev3.reshape(L, K)
    q_dec = (q * jnp.exp(b - b_prev)).astype(BF16)
    k_dec = (kk * jnp.exp(b_end - b)).astype(BF16)
    blk_dec = jnp.exp(b_end - b_prev)

    v_t = i_ref[...].astype(F32).T
    lane_blk = lax.broadcasted_iota(jnp.int32, (K, L), 1) // C
    outer = [jnp.dot(jnp.where(lane_blk == i, v_t, 0.0).astype(BF16), k_dec, preferred_element_type=F32)
             for i in range(NB)]
    s = state_scr[...]
    o_t = jnp.zeros((K, L), F32)
    for i in range(NB):
        r = lax.dot_general(s.astype(BF16), q_dec, nt, preferred_element_type=F32)
        o_t = jnp.where(lane_blk == i, r, o_t)
        s = s * blk_dec[i * C:i * C + 1, :] + outer[i]
    state_scr[...] = s

    q_e = jnp.dot(q.T.astype(BF16), rep_ref[...], preferred_element_type=F32)
    k_e = jnp.dot(kk.T.astype(BF16), tile_ref[...], preferred_element_type=F32)
    dif = dif_ref[...]
    seg = sum(jnp.dot(part, dif, preferred_element_type=F32) for part in _split3(b.T))
    j = lax.broadcasted_iota(jnp.int32, (K, HGRN_PAIRS), 1)
    ordered = (j % C) <= ((j // C) % C)
    att = jnp.sum(q_e * k_e * jnp.exp(jnp.where(ordered, seg, -jnp.inf)), axis=0, keepdims=True)
    v_e = jnp.dot(v_t.astype(BF16), tile_ref[...], preferred_element_type=F32)
    o_t = o_t + jnp.dot((v_e * att).astype(BF16), fold_ref[...], preferred_element_type=F32)

    o = _rmsnorm(o_t.T, gnorm_ref[...]) * _silu(g_ref[...].astype(F32))
    o_ref[...] = o.astype(o_ref.dtype)


def _hgrn_pair_maps():
    c = HGRN_BLOCK
    j = jnp.arange(HGRN_PAIRS)
    t_of = j // c
    s_of = (t_of // c) * c + j % c
    t = jnp.arange(CHUNK)[:, None]
    rep = (t == t_of[None, :]).astype(F32)
    tile = (t == s_of[None, :]).astype(F32)
    return rep.astype(BF16), tile.astype(BF16), (rep - tile).astype(BF16), rep.T.astype(BF16)


def _hgrn_layer(x, bsz, g_mix, w_in, lower_bound, norm_g, w_out):
    t, d = x.shape
    nc = t // bsz // CHUNK
    hd = HGRN_HEAD_DIM
    qfig = _norm_matmul(x, g_mix, w_in.astype(BF16))
    part = lambda k: pl.BlockSpec((CHUNK, hd), lambda b, h, c: (b * nc + c, k * HGRN_N_HEADS + h))
    const = lambda shape: pl.BlockSpec(shape, lambda b, h, c: (0, 0))
    rep, tile, dif, fold = _hgrn_pair_maps()
    o = pl.pallas_call(
        _hgrn_kernel,
        grid=(bsz, HGRN_N_HEADS, nc),
        in_specs=[part(0), part(1), part(2), part(3),
                  pl.BlockSpec((1, hd), lambda b, h, c: (0, h)), const((1, hd)),
                  const(rep.shape), const(tile.shape), const(dif.shape), const(fold.shape)],
        out_specs=pl.BlockSpec((CHUNK, hd), lambda b, h, c: (b * nc + c, h)),
        out_shape=jax.ShapeDtypeStruct((t, d), BF16),
        scratch_shapes=[pltpu.VMEM((hd, hd), F32)],
        compiler_params=_params("parallel", "parallel", "arbitrary"),
        name="hgrn_scan",
    )(qfig, qfig, qfig, qfig, lower_bound.reshape(1, d).astype(F32), norm_g.reshape(1, hd).astype(F32),
      rep, tile, dif, fold)
    return _matmul_residual(x, o, w_out.astype(BF16))


def _topk_rows(s, k, payload=None):
    r = s.shape[0]
    rows = lax.broadcasted_iota(jnp.int32, s.shape, 0)
    vals, picks = [], []
    for _ in range(k):
        m = jnp.max(s, axis=0, keepdims=True)
        pos = jnp.min(jnp.where(s == m, rows, r), axis=0, keepdims=True)
        hit = rows == pos
        vals.append(m)
        picks.append(pos if payload is None else jnp.sum(jnp.where(hit, payload, 0), axis=0, keepdims=True))
        s = jnp.where(hit, -jnp.inf, s)
    return jnp.concatenate(vals, axis=0), jnp.concatenate(picks, axis=0)


def _peer_route_kernel(x_ref, g_ref, wq_ref, keys_ref, h_ref, idx_ref, gate_ref):
    h = _rmsnorm(x_ref[...], g_ref[...])
    h_ref[...] = h
    q_t = lax.dot_general(wq_ref[...], h.astype(BF16), (((1,), (1,)), ((), ())),
                          preferred_element_type=F32)
    idx_rows, gate_rows = [], []
    for head in range(PEER_N_HEADS):
        top_s, top_i = [], []
        for c in range(2):
            hc = head * 2 + c
            q_hc = q_t[hc * PEER_D_HALF:(hc + 1) * PEER_D_HALF, :].astype(BF16)
            sc = jnp.dot(keys_ref[hc], q_hc, preferred_element_type=F32)
            s, i = _topk_rows(sc, PEER_TOPK)
            top_s.append(s)
            top_i.append(i)
        cand_s = jnp.concatenate([top_s[0][a:a + 1, :] + top_s[1] for a in range(PEER_TOPK)], axis=0)
        cand_i = jnp.concatenate([top_i[0][a:a + 1, :] * PEER_N_KEYS + top_i[1] for a in range(PEER_TOPK)],
                                 axis=0)
        best_s, best_i = _topk_rows(cand_s, PEER_TOPK, payload=cand_i)
        e = jnp.exp(best_s - best_s[0:1, :])
        gate_rows.append(e / jnp.sum(e, axis=0, keepdims=True))
        idx_rows.append(best_i)
    idx_ref[...] = jnp.concatenate(idx_rows, axis=0).astype(F32).T.astype(jnp.int32) * ROW_SUB
    gate_ref[...] = jnp.concatenate(gate_rows, axis=0).T


def _peer_route(x, g, w_query, sub_keys, tn=128):
    t, d = x.shape
    nq = w_query.shape[1]
    keys = sub_keys.reshape(2 * PEER_N_HEADS, PEER_N_KEYS, PEER_D_HALF).astype(BF16)
    return pl.pallas_call(
        _peer_route_kernel,
        grid=(t // tn,),
        in_specs=[pl.BlockSpec((tn, d), lambda i: (i, 0)),
                  pl.BlockSpec((1, d), lambda i: (0, 0)),
                  pl.BlockSpec((nq, d), lambda i: (0, 0)),
                  pl.BlockSpec(keys.shape, lambda i: (0, 0, 0))],
        out_specs=[pl.BlockSpec((tn, d), lambda i: (i, 0)),
                   pl.BlockSpec((tn, PEER_SLOTS), lambda i: (i, 0)),
                   pl.BlockSpec((tn, PEER_SLOTS), lambda i: (i, 0))],
        out_shape=[jax.ShapeDtypeStruct((t, d), F32),
                   jax.ShapeDtypeStruct((t, PEER_SLOTS), jnp.int32),
                   jax.ShapeDtypeStruct((t, PEER_SLOTS), F32)],
        compiler_params=_params("parallel"),
        name="peer_route",
    )(x, g.reshape(1, d), w_query.T.astype(BF16), keys)


SLAB_ROWS = PEER_SLOTS * ROW_SUB
SLAB_COLS = 2 * SLAB_ROWS
GATHER_UNROLL = 16


def _pack_table(t):
    b = lax.bitcast_convert_type(t.astype(BF16), jnp.uint16).astype(jnp.uint32)
    w = b[:, :ROW_WORDS] | (b[:, ROW_WORDS:] << 16)
    return w.reshape(-1, LANES)


def _slab_maps():
    col = jnp.arange(SLAB_COLS)
    row = jnp.arange(SUBLANES)[:, None]
    diag = (row == ((col // 2) % ROW_SUB + ROW_SUB * (col % 2))[None, :]).astype(F32)
    spread = (jnp.arange(PEER_SLOTS)[:, None] == (col // SUBLANES)[None, :]).astype(BF16)
    return diag, spread


def _gather_rows(idx_ref, tab_ref, slab, t):
    base = t * PEER_SLOTS
    for p in range(PEER_SLOTS):
        row = pl.multiple_of(idx_ref[base + p], ROW_SUB)
        slab[p * ROW_SUB:(p + 1) * ROW_SUB, :] = tab_ref[pl.ds(row, ROW_SUB), :]


def _pipelined_tokens(tb, gather, finish, slab_a, slab_b):
    gather(slab_a, 0)
    slabs = (slab_a, slab_b)

    def group(i, carry):
        t0 = GATHER_UNROLL * i
        for k in range(GATHER_UNROLL):
            gather(slabs[(k + 1) % 2], jnp.minimum(t0 + k + 1, tb - 1))
            finish(slabs[k % 2], t0 + k)
        return carry

    lax.fori_loop(0, tb // GATHER_UNROLL, group, 0)


_NT = (((1,), (1,)), ((), ()))


def _peer_u_kernel(idx_ref, h_ref, gate_ref, tab_ref, diag_ref, spread_ref, spread_t_ref, crep_ref,
                   slab_a, slab_b, rbuf):
    tb = h_ref.shape[0]
    diag = diag_ref[...]

    def finish(slab, t):
        w = pltpu.bitcast(slab[...], BF16)
        r = lax.dot_general(h_ref[t].astype(BF16), w, _NT, preferred_element_type=F32)
        rbuf[pl.ds(t, 1), :] = jnp.sum(r * diag, axis=0, keepdims=True)

    _pipelined_tokens(tb, functools.partial(_gather_rows, idx_ref, tab_ref), finish, slab_a, slab_b)
    spread_t = spread_t_ref[...]
    act = sum(jnp.dot(part, spread_t, preferred_element_type=F32) for part in _split3(rbuf[...]))
    gelu = 0.5 * act * (1.0 + lax.erf(act * (2.0 ** -0.5)))
    c = (gate_ref[...] * gelu).astype(BF16)
    crep_ref[...] = jnp.dot(c, spread_ref[...], preferred_element_type=F32)


def _peer_v_kernel(idx_ref, crep_ref, x_ref, tab_ref, diag_ref, o_ref, slab_a, slab_b):
    tb = x_ref.shape[0]
    diag = diag_ref[...]

    def finish(slab, t):
        w = pltpu.bitcast(slab[...], BF16)
        c = (crep_ref[pl.ds(t, 1), :] * diag).astype(BF16)
        o_ref[t] = x_ref[t] + jnp.dot(c, w, preferred_element_type=F32)

    _pipelined_tokens(tb, functools.partial(_gather_rows, idx_ref, tab_ref), finish, slab_a, slab_b)


def _peer_gather(x3, h3, rows, gate, u_packed, v_packed, tb=128):
    t = x3.shape[0]
    diag, spread = _slab_maps()
    spread_t = spread.T
    rows1 = rows.reshape(-1)
    grid = (t // tb,)
    tok3 = pl.BlockSpec((tb, SUBLANES, LANES), lambda i: (i, 0, 0))
    tok2 = pl.BlockSpec((tb, PEER_SLOTS), lambda i: (i, 0))
    rep2 = pl.BlockSpec((tb, SLAB_COLS), lambda i: (i, 0))
    smem1 = pl.BlockSpec((tb * PEER_SLOTS,), lambda i: (i,), memory_space=pltpu.SMEM)
    table = pl.BlockSpec(u_packed.shape, lambda i: (0, 0), pipeline_mode=pl.Buffered(1))
    const = lambda a: pl.BlockSpec(a.shape, lambda i: (0, 0))
    slab = pltpu.VMEM((SLAB_ROWS, LANES), jnp.uint32)
    crep = pl.pallas_call(
        _peer_u_kernel,
        grid=grid,
        in_specs=[smem1, tok3, tok2, table, const(diag), const(spread), const(spread_t)],
        out_specs=rep2,
        out_shape=jax.ShapeDtypeStruct((t, SLAB_COLS), F32),
        scratch_shapes=[slab, slab, pltpu.VMEM((tb, SLAB_COLS), F32)],
        compiler_params=_params("arbitrary"),
        name="peer_u",
    )(rows1, h3, gate, u_packed, diag, spread, spread_t)
    return pl.pallas_call(
        _peer_v_kernel,
        grid=grid,
        in_specs=[smem1, rep2, tok3, table, const(diag)],
        out_specs=tok3,
        out_shape=jax.ShapeDtypeStruct((t, SUBLANES, LANES), F32),
        scratch_shapes=[slab, slab],
        compiler_params=_params("arbitrary"),
        name="peer_v",
    )(rows1, crep, x3, v_packed, diag)


def _peer_layer(x, g, w_query, sub_keys, expert_u, expert_v):
    t, d = x.shape
    h, rows, gate = _peer_route(x, g, w_query, sub_keys)
    x3 = x.reshape(t, SUBLANES, LANES)
    h3 = h.reshape(t, SUBLANES, LANES)
    y3 = _peer_gather(x3, h3, rows, gate, _pack_table(expert_u), _pack_table(expert_v))
    return y3.reshape(t, d)


def _hgrn_lower_bound(lb_logits, layer):
    p = jax.nn.softmax(lb_logits.astype(F32), axis=0)
    return jnp.cumsum(p, axis=0)[layer] - p[0]


def kernel(x, mix_norm_g, ffn_norm_g, final_norm_g, ssd_w_in, ssd_conv_w, ssd_conv_b, ssd_dt_bias, ssd_a_log, ssd_d_skip, ssd_norm_g, ssd_w_out, hgrn_w_in, hgrn_lb_logits, hgrn_norm_g, hgrn_w_out, peer_w_query, peer_sub_keys, peer_u, peer_v):
    bsz, s, d = x.shape
    depth = mix_norm_g.shape[0]
    xf = x.reshape(bsz * s, d)
    for layer in range(depth):
        j = layer // 2
        if layer % 2 == 0:
            xf = _ssd_layer(xf, bsz, mix_norm_g[layer], ssd_w_in[j], ssd_conv_w[j], ssd_conv_b[j],
                            ssd_dt_bias[j], ssd_a_log[j], ssd_d_skip[j], ssd_norm_g[j], ssd_w_out[j])
        else:
            lb = _hgrn_lower_bound(hgrn_lb_logits, layer)
            xf = _hgrn_layer(xf, bsz, mix_norm_g[layer], hgrn_w_in[j], lb, hgrn_norm_g[j], hgrn_w_out[j])
        xf = _peer_layer(xf, ffn_norm_g[layer], peer_w_query[layer], peer_sub_keys[layer],
                         peer_u[layer], peer_v[layer])
    return _final_norm(xf, final_norm_g).reshape(bsz, s, d)
```

```python
import functools

import jax
import jax.numpy as jnp
from jax import lax
from jax.experimental import pallas as pl
from jax.experimental.pallas import tpu as pltpu

F32 = jnp.float32
BF16 = jnp.bfloat16
HIGHEST = lax.Precision.HIGHEST

LANES = 128
SUBLANES = 8
VMEM_LIMIT = 56 * 1024 * 1024

D_MODEL = 1024
NORM_EPS = 1e-6

SSD_D_INNER = 2048
SSD_HEAD_DIM = 64
SSD_N_HEADS = 32
SSD_N_GROUPS = 4
SSD_HEADS_PER_GROUP = 8
SSD_D_STATE = 128
SSD_CONV_WIDTH = 4
SSD_CONV_DIM = SSD_D_INNER + 2 * SSD_N_GROUPS * SSD_D_STATE
SSD_GROUP_DIM = SSD_D_INNER // SSD_N_GROUPS
CHUNK = 128

HGRN_N_HEADS = 8
HGRN_HEAD_DIM = 128

PEER_N_KEYS = 128
PEER_N_HEADS = 8
PEER_D_HALF = 128
PEER_TOPK = 16
PEER_SLOTS = PEER_N_HEADS * PEER_TOPK
ROW_WORDS = D_MODEL // 2
ROW_SUB = ROW_WORDS // LANES


def _rmsnorm(x, g):
    r = lax.rsqrt(jnp.mean(x * x, axis=-1, keepdims=True) + NORM_EPS)
    return (x * r) * g


def _silu(x):
    return x * (1.0 / (1.0 + jnp.exp(-x)))


def _sigmoid(x):
    return 1.0 / (1.0 + jnp.exp(-x))


def _params(*sem):
    return pltpu.CompilerParams(dimension_semantics=sem, vmem_limit_bytes=VMEM_LIMIT)


def _norm_matmul_kernel(x_ref, g_ref, w_ref, o_ref, h_scr):
    @pl.when(pl.program_id(1) == 0)
    def _():
        h_scr[...] = _rmsnorm(x_ref[...], g_ref[...]).astype(BF16)

    o_ref[...] = jnp.dot(h_scr[...], w_ref[...], preferred_element_type=F32).astype(o_ref.dtype)


def _norm_matmul(x, g, w, tm=512, tn=512, out_dtype=BF16):
    t, d = x.shape
    n = w.shape[1]
    return pl.pallas_call(
        _norm_matmul_kernel,
        grid=(t // tm, n // tn),
        in_specs=[pl.BlockSpec((tm, d), lambda i, j: (i, 0)),
                  pl.BlockSpec((1, d), lambda i, j: (0, 0)),
                  pl.BlockSpec((d, tn), lambda i, j: (0, j))],
        out_specs=pl.BlockSpec((tm, tn), lambda i, j: (i, j)),
        out_shape=jax.ShapeDtypeStruct((t, n), out_dtype),
        scratch_shapes=[pltpu.VMEM((tm, d), BF16)],
        compiler_params=_params("parallel", "arbitrary"),
        name="norm_matmul",
    )(x, g.reshape(1, d), w)


def _matmul_residual_kernel(x_ref, a_ref, w_ref, o_ref):
    o_ref[...] = x_ref[...] + jnp.dot(a_ref[...], w_ref[...], preferred_element_type=F32)


def _matmul_residual(x, a, w, tm=512):
    t, d = x.shape
    k = a.shape[1]
    return pl.pallas_call(
        _matmul_residual_kernel,
        grid=(t // tm,),
        in_specs=[pl.BlockSpec((tm, d), lambda i: (i, 0)),
                  pl.BlockSpec((tm, k), lambda i: (i, 0)),
                  pl.BlockSpec((k, d), lambda i: (0, 0))],
        out_specs=pl.BlockSpec((tm, d), lambda i: (i, 0)),
        out_shape=jax.ShapeDtypeStruct((t, d), F32),
        compiler_params=_params("parallel"),
        name="matmul_residual",
    )(x, a, w)


def _final_norm_kernel(x_ref, g_ref, o_ref):
    o_ref[...] = _rmsnorm(x_ref[...], g_ref[...])


def _final_norm(x, g, tm=1024):
    t, d = x.shape
    return pl.pallas_call(
        _final_norm_kernel,
        grid=(t // tm,),
        in_specs=[pl.BlockSpec((tm, d), lambda i: (i, 0)),
                  pl.BlockSpec((1, d), lambda i: (0, 0))],
        out_specs=pl.BlockSpec((tm, d), lambda i: (i, 0)),
        out_shape=jax.ShapeDtypeStruct((t, d), F32),
        compiler_params=_params("parallel"),
        name="final_norm",
    )(x, g.reshape(1, d))


def _ssd_kernel(x_ref, zx_ref, gmix_ref, wdt_ref, dtb_ref, alog_ref, convw_ref, convb_ref,
                dskip_ref, expand_ref, gnorm_ref, wout_ref, o_ref, state_scr, tail_scr):
    L = CHUNK
    G, N, P = SSD_N_GROUPS, SSD_D_STATE, SSD_HEAD_DIM
    GD = SSD_GROUP_DIM

    @pl.when(pl.program_id(1) == 0)
    def _():
        state_scr[...] = jnp.zeros_like(state_scr)
        tail_scr[...] = jnp.zeros_like(tail_scr)

    x = x_ref[...]
    h = _rmsnorm(x, gmix_ref[...]).astype(BF16)
    dt_raw = jnp.dot(h, wdt_ref[...], preferred_element_type=F32) + dtb_ref[...]
    dt = jnp.maximum(dt_raw, 0.0) + jnp.log(1.0 + jnp.exp(-jnp.abs(dt_raw)))
    a = -jnp.exp(alog_ref[...])
    row = lax.broadcasted_iota(jnp.int32, (L, L), 0)
    col = lax.broadcasted_iota(jnp.int32, (L, L), 1)
    causal = row >= col
    tril = causal.astype(F32)
    acs = jnp.dot(tril, dt * a, preferred_element_type=F32, precision=HIGHEST)
    acs_t = acs.T
    dt_t = dt.T
    exp_acs = jnp.exp(acs)
    tail = jnp.exp(acs[L - 1:L, :] - acs) * dt
    both = jnp.concatenate([exp_acs, tail], axis=0)
    both_e = jnp.dot(both, expand_ref[...], preferred_element_type=F32, precision=HIGHEST)
    exp_acs_e = both_e[:L]
    tail_e = both_e[L:]

    zx = zx_ref[...]
    z = zx[:, :SSD_D_INNER].astype(F32)
    xbc = zx[:, SSD_D_INNER:].astype(F32)
    ext = jnp.concatenate([tail_scr[...], xbc], axis=0)
    tail_scr[...] = xbc[L - SUBLANES:, :]
    conv = convb_ref[...] + convw_ref[SSD_CONV_WIDTH - 1:SSD_CONV_WIDTH, :] * xbc
    for k in range(SSD_CONV_WIDTH - 1):
        off = SUBLANES - (SSD_CONV_WIDTH - 1) + k
        conv = conv + convw_ref[k:k + 1, :] * ext[off:off + L, :]
    xbc = _silu(conv)
    xs = xbc[:, :SSD_D_INNER]
    bm = xbc[:, SSD_D_INNER:SSD_D_INNER + G * N]
    cm = xbc[:, SSD_D_INNER + G * N:]

    xs_tail = (xs * tail_e).astype(BF16)
    xs_b = xs.astype(BF16)
    y_parts = []
    for g in range(G):
        b_g = bm[:, g * N:(g + 1) * N].astype(BF16)
        c_g = cm[:, g * N:(g + 1) * N].astype(BF16)
        cb = lax.dot_general(c_g, b_g, (((1,), (1,)), ((), ())), preferred_element_type=F32)
        st = state_scr[g]
        y_inter = jnp.dot(c_g, st.astype(BF16), preferred_element_type=F32)
        y_g = y_inter * exp_acs_e[:, g * GD:(g + 1) * GD]
        intra = []
        for j in range(SSD_HEADS_PER_GROUP):
            hh = g * SSD_HEADS_PER_GROUP + j
            seg = acs[:, hh:hh + 1] - acs_t[hh:hh + 1, :]
            decay = jnp.exp(jnp.where(causal, seg, -jnp.inf))
            w = (cb * decay * dt_t[hh:hh + 1, :]).astype(BF16)
            intra.append(jnp.dot(w, xs_b[:, hh * P:(hh + 1) * P], preferred_element_type=F32))
        y_g = y_g + jnp.concatenate(intra, axis=1)
        y_parts.append(y_g)
        upd = lax.dot_general(b_g, xs_tail[:, g * GD:(g + 1) * GD], (((0,), (0,)), ((), ())),
                              preferred_element_type=F32)
        state_scr[g] = st * exp_acs_e[L - 1:L, g * GD:(g + 1) * GD] + upd
    y = jnp.concatenate(y_parts, axis=1)
    y = y + xs * dskip_ref[...]
    y = y * _silu(z)
    normed = []
    for g in range(G):
        y_g = y[:, g * GD:(g + 1) * GD]
        normed.append(_rmsnorm(y_g, gnorm_ref[:, g * GD:(g + 1) * GD]))
    yn = jnp.concatenate(normed, axis=1).astype(BF16)
    o_ref[...] = x + jnp.dot(yn, wout_ref[...], preferred_element_type=F32)


def _pad_lanes(v):
    return jnp.pad(v.astype(F32), (0, LANES - v.shape[0])).reshape(1, LANES)


def _ssd_layer(x, bsz, g_mix, w_in, conv_w, conv_b, dt_bias, a_log, d_skip, norm_g, w_out):
    t, d = x.shape
    nc = t // bsz // CHUNK
    w_zx = w_in[:, :SSD_D_INNER + SSD_CONV_DIM].astype(BF16)
    w_dt = jnp.pad(w_in[:, SSD_D_INNER + SSD_CONV_DIM:], ((0, 0), (0, LANES - SSD_N_HEADS))).astype(BF16)
    zx = _norm_matmul(x, g_mix, w_zx)
    head_of_lane = jnp.arange(SSD_D_INNER) // SSD_HEAD_DIM
    expand = (jnp.arange(LANES)[:, None] == head_of_lane[None, :]).astype(F32)
    dskip_e = jnp.repeat(d_skip.astype(F32), SSD_HEAD_DIM).reshape(1, SSD_D_INNER)
    const = lambda shape: pl.BlockSpec(shape, lambda b, c: (0,) * len(shape))
    tok = lambda width: pl.BlockSpec((CHUNK, width), lambda b, c: (b * nc + c, 0))
    return pl.pallas_call(
        _ssd_kernel,
        grid=(bsz, nc),
        in_specs=[tok(d), tok(SSD_D_INNER + SSD_CONV_DIM), const((1, d)), const((d, LANES)),
                  const((1, LANES)), const((1, LANES)), const((SSD_CONV_WIDTH, SSD_CONV_DIM)),
                  const((1, SSD_CONV_DIM)), const((1, SSD_D_INNER)), const((LANES, SSD_D_INNER)),
                  const((1, SSD_D_INNER)), const((SSD_D_INNER, d))],
        out_specs=tok(d),
        out_shape=jax.ShapeDtypeStruct((t, d), F32),
        scratch_shapes=[pltpu.VMEM((SSD_N_GROUPS, SSD_D_STATE, SSD_GROUP_DIM), F32),
                        pltpu.VMEM((SUBLANES, SSD_CONV_DIM), F32)],
        compiler_params=_params("parallel", "arbitrary"),
        name="ssd_scan",
    )(x, zx, g_mix.reshape(1, d), w_dt, _pad_lanes(dt_bias), _pad_lanes(a_log), conv_w.astype(F32),
      conv_b.reshape(1, -1).astype(F32), dskip_e, expand, norm_g.reshape(1, -1).astype(F32),
      w_out.astype(BF16))


HGRN_BLOCK = SUBLANES
HGRN_PAIRS = CHUNK * HGRN_BLOCK


def _split3(x):
    x1 = x.astype(BF16)
    r1 = x - x1.astype(F32)
    x2 = r1.astype(BF16)
    x3 = (r1 - x2.astype(F32)).astype(BF16)
    return x1, x2, x3


def _hgrn_kernel(q_ref, f_ref, i_ref, g_ref, lb_ref, gnorm_ref, rep_ref, tile_ref, dif_ref, fold_ref,
                 o_ref, state_scr):
    L, C = CHUNK, HGRN_BLOCK
    NB = L // C
    K = HGRN_HEAD_DIM
    nt = (((1,), (1,)), ((), ()))

    @pl.when(pl.program_id(2) == 0)
    def _():
        state_scr[...] = jnp.zeros_like(state_scr)

    lb = lb_ref[...]
    fl = f_ref[...].astype(F32)
    logf = jnp.log(lb + (1.0 - lb) * _sigmoid(fl))
    kk = (1.0 - lb) * _sigmoid(-fl)
    q = q_ref[...].astype(F32)
    row = lax.broadcasted_iota(jnp.int32, (L, L), 0)
    col = lax.broadcasted_iota(jnp.int32, (L, L), 1)
    b = jnp.dot((row >= col).astype(F32), logf, preferred_element_type=F32, precision=HIGHEST)
    b3 = b.reshape(NB, C, K)
    b_end3 = jnp.broadcast_to(b3[:, C - 1:C, :], b3.shape)
    b_prev3 = jnp.concatenate([jnp.zeros((1, C, K), F32), b_end3[:NB - 1]], axis=0)
    b_end = b_end3.reshape(L, K)
    b_prev = b_prev3.reshape(L, K)
    q_dec = (q * jnp.exp(b - b_prev)).astype(BF16)
    k_dec = (kk * jnp.exp(b_end - b)).astype(BF16)
    blk_dec = jnp.exp(b_end - b_prev)

    v_t = i_ref[...].astype(F32).T
    lane_blk = lax.broadcasted_iota(jnp.int32, (K, L), 1) // C
    outer = [jnp.dot(jnp.where(lane_blk == i, v_t, 0.0).astype(BF16), k_dec, preferred_element_type=F32)
             for i in range(NB)]
    s = state_scr[...]
    o_t = jnp.zeros((K, L), F32)
    for i in range(NB):
        r = lax.dot_general(s.astype(BF16), q_dec, nt, preferred_element_type=F32)
        o_t = jnp.where(lane_blk == i, r, o_t)
        s = s * blk_dec[i * C:i * C + 1, :] + outer[i]
    state_scr[...] = s

    q_e = jnp.dot(q.T.astype(BF16), rep_ref[...], preferred_element_type=F32)
    k_e = jnp.dot(kk.T.astype(BF16), tile_ref[...], preferred_element_type=F32)
    dif = dif_ref[...]
    seg = sum(jnp.dot(part, dif, preferred_element_type=F32) for part in _split3(b.T))
    j = lax.broadcasted_iota(jnp.int32, (K, HGRN_PAIRS), 1)
    ordered = (j % C) <= ((j // C) % C)
    att = jnp.sum(q_e * k_e * jnp.exp(jnp.where(ordered, seg, -jnp.inf)), axis=0, keepdims=True)
    v_e = jnp.dot(v_t.astype(BF16), tile_ref[...], preferred_element_type=F32)
    o_t = o_t + jnp.dot((v_e * att).astype(BF16), fold_ref[...], preferred_element_type=F32)

    o = _rmsnorm(o_t.T, gnorm_ref[...]) * _silu(g_ref[...].astype(F32))
    o_ref[...] = o.astype(o_ref.dtype)


def _hgrn_pair_maps():
    c = HGRN_BLOCK
    j = jnp.arange(HGRN_PAIRS)
    t_of = j // c
    s_of = (t_of // c) * c + j % c
    t = jnp.arange(CHUNK)[:, None]
    rep = (t == t_of[None, :]).astype(F32)
    tile = (t == s_of[None, :]).astype(F32)
    return rep.astype(BF16), tile.astype(BF16), (rep - tile).astype(BF16), rep.T.astype(BF16)


def _hgrn_layer(x, bsz, g_mix, w_in, lower_bound, norm_g, w_out):
    t, d = x.shape
    nc = t // bsz // CHUNK
    hd = HGRN_HEAD_DIM
    qfig = _norm_matmul(x, g_mix, w_in.astype(BF16))
    part = lambda k: pl.BlockSpec((CHUNK, hd), lambda b, h, c: (b * nc + c, k * HGRN_N_HEADS + h))
    const = lambda shape: pl.BlockSpec(shape, lambda b, h, c: (0, 0))
    rep, tile, dif, fold = _hgrn_pair_maps()
    o = pl.pallas_call(
        _hgrn_kernel,
        grid=(bsz, HGRN_N_HEADS, nc),
        in_specs=[part(0), part(1), part(2), part(3),
                  pl.BlockSpec((1, hd), lambda b, h, c: (0, h)), const((1, hd)),
                  const(rep.shape), const(tile.shape), const(dif.shape), const(fold.shape)],
        out_specs=pl.BlockSpec((CHUNK, hd), lambda b, h, c: (b * nc + c, h)),
        out_shape=jax.ShapeDtypeStruct((t, d), BF16),
        scratch_shapes=[pltpu.VMEM((hd, hd), F32)],
        compiler_params=_params("parallel", "parallel", "arbitrary"),
        name="hgrn_scan",
    )(qfig, qfig, qfig, qfig, lower_bound.reshape(1, d).astype(F32), norm_g.reshape(1, hd).astype(F32),
      rep, tile, dif, fold)
    return _matmul_residual(x, o, w_out.astype(BF16))


def _topk_rows(s, k, payload=None):
    r = s.shape[0]
    rows = lax.broadcasted_iota(jnp.int32, s.shape, 0).astype(F32)
    vals, picks = [], []
    for _ in range(k):
        m = jnp.max(s, axis=0, keepdims=True)
        at_max = jnp.where(s == m, rows, float(r))
        pos = jnp.min(at_max, axis=0, keepdims=True)
        hit = at_max == pos
        vals.append(m)
        picks.append(pos if payload is None else jnp.sum(jnp.where(hit, payload, 0.0), axis=0, keepdims=True))
        s = jnp.where(hit, -jnp.inf, s)
    return jnp.concatenate(vals, axis=0), jnp.concatenate(picks, axis=0)


def _product_candidates(first, second, combine):
    half = PEER_TOPK // 2
    parts = [combine(first[0:1, :], second[0:half, :]), combine(first[0:1, :], second[half:, :])]
    parts += [combine(first[a:a + 1, :], second[0:half, :]) for a in range(1, half)]
    parts.append(combine(first[half:, :], second[0:1, :]))
    return jnp.concatenate(parts, axis=0)


def _peer_route_kernel(x_ref, g_ref, wq_ref, keys_ref, h_ref, idx_ref, gate_ref):
    h = _rmsnorm(x_ref[...], g_ref[...])
    h_ref[...] = h
    q_t = lax.dot_general(wq_ref[...], h.astype(BF16), (((1,), (1,)), ((), ())),
                          preferred_element_type=F32)
    idx_rows, gate_rows = [], []
    for head in range(PEER_N_HEADS):
        top_s, top_i = [], []
        for c in range(2):
            hc = head * 2 + c
            q_hc = q_t[hc * PEER_D_HALF:(hc + 1) * PEER_D_HALF, :].astype(BF16)
            sc = jnp.dot(keys_ref[hc], q_hc, preferred_element_type=F32)
            s, i = _topk_rows(sc, PEER_TOPK)
            top_s.append(s)
            top_i.append(i)
        cand_s = _product_candidates(top_s[0], top_s[1], lambda a, b: a + b)
        cand_i = _product_candidates(top_i[0], top_i[1], lambda a, b: a * float(PEER_N_KEYS) + b)
        best_s, best_i = _topk_rows(cand_s, PEER_TOPK, payload=cand_i)
        e = jnp.exp(best_s - best_s[0:1, :])
        gate_rows.append(e / jnp.sum(e, axis=0, keepdims=True))
        idx_rows.append(best_i)
    idx_ref[...] = jnp.concatenate(idx_rows, axis=0).T.astype(jnp.int32) * ROW_SUB
    gate_ref[...] = jnp.concatenate(gate_rows, axis=0).T


def _peer_route(x, g, w_query, sub_keys, tn=128):
    t, d = x.shape
    nq = w_query.shape[1]
    keys = sub_keys.reshape(2 * PEER_N_HEADS, PEER_N_KEYS, PEER_D_HALF).astype(BF16)
    return pl.pallas_call(
        _peer_route_kernel,
        grid=(t // tn,),
        in_specs=[pl.BlockSpec((tn, d), lambda i: (i, 0)),
                  pl.BlockSpec((1, d), lambda i: (0, 0)),
                  pl.BlockSpec((nq, d), lambda i: (0, 0)),
                  pl.BlockSpec(keys.shape, lambda i: (0, 0, 0))],
        out_specs=[pl.BlockSpec((tn, d), lambda i: (i, 0)),
                   pl.BlockSpec((tn, PEER_SLOTS), lambda i: (i, 0)),
                   pl.BlockSpec((tn, PEER_SLOTS), lambda i: (i, 0))],
        out_shape=[jax.ShapeDtypeStruct((t, d), F32),
                   jax.ShapeDtypeStruct((t, PEER_SLOTS), jnp.int32),
                   jax.ShapeDtypeStruct((t, PEER_SLOTS), F32)],
        compiler_params=_params("parallel"),
        name="peer_route",
    )(x, g.reshape(1, d), w_query.T.astype(BF16), keys)


SLAB_ROWS = PEER_SLOTS * ROW_SUB
SLAB_COLS = 2 * SLAB_ROWS
GATHER_UNROLL = 16


def _pack_table(t):
    b = lax.bitcast_convert_type(t.astype(BF16), jnp.uint16).astype(jnp.uint32)
    w = b[:, :ROW_WORDS] | (b[:, ROW_WORDS:] << 16)
    return w.reshape(-1, LANES)


def _slab_maps():
    col = jnp.arange(SLAB_COLS)
    row = jnp.arange(SUBLANES)[:, None]
    diag = (row == ((col // 2) % ROW_SUB + ROW_SUB * (col % 2))[None, :]).astype(F32)
    spread = (jnp.arange(PEER_SLOTS)[:, None] == (col // SUBLANES)[None, :]).astype(BF16)
    return diag, spread


def _gather_rows(idx_ref, tab_ref, slab, t):
    base = t * PEER_SLOTS
    for p in range(PEER_SLOTS):
        row = pl.multiple_of(idx_ref[base + p], ROW_SUB)
        slab[p * ROW_SUB:(p + 1) * ROW_SUB, :] = tab_ref[pl.ds(row, ROW_SUB), :]


def _pipelined_tokens(tb, gather, finish, slab_a, slab_b):
    gather(slab_a, 0)
    slabs = (slab_a, slab_b)

    def group(i, carry):
        t0 = GATHER_UNROLL * i
        for k in range(GATHER_UNROLL):
            gather(slabs[(k + 1) % 2], jnp.minimum(t0 + k + 1, tb - 1))
            finish(slabs[k % 2], t0 + k)
        return carry

    lax.fori_loop(0, tb // GATHER_UNROLL, group, 0)


_NT = (((1,), (1,)), ((), ()))


def _peer_u_kernel(idx_ref, h_ref, gate_ref, tab_ref, diag_ref, spread_ref, spread_t_ref, crep_ref,
                   slab_a, slab_b, rbuf):
    tb = h_ref.shape[0]
    diag = diag_ref[...]

    def finish(slab, t):
        w = pltpu.bitcast(slab[...], BF16)
        r = lax.dot_general(h_ref[t].astype(BF16), w, _NT, preferred_element_type=F32)
        rbuf[pl.ds(t, 1), :] = jnp.sum(r * diag, axis=0, keepdims=True)

    _pipelined_tokens(tb, functools.partial(_gather_rows, idx_ref, tab_ref), finish, slab_a, slab_b)
    spread_t = spread_t_ref[...]
    act = sum(jnp.dot(part, spread_t, preferred_element_type=F32) for part in _split3(rbuf[...]))
    gelu = 0.5 * act * (1.0 + lax.erf(act * (2.0 ** -0.5)))
    c = (gate_ref[...] * gelu).astype(BF16)
    crep_ref[...] = jnp.dot(c, spread_ref[...], preferred_element_type=F32)


def _peer_v_kernel(idx_ref, crep_ref, x_ref, tab_ref, diag_ref, o_ref, slab_a, slab_b):
    tb = x_ref.shape[0]
    diag = diag_ref[...]

    def finish(slab, t):
        w = pltpu.bitcast(slab[...], BF16)
        c = (crep_ref[pl.ds(t, 1), :] * diag).astype(BF16)
        o_ref[t] = x_ref[t] + jnp.dot(c, w, preferred_element_type=F32)

    _pipelined_tokens(tb, functools.partial(_gather_rows, idx_ref, tab_ref), finish, slab_a, slab_b)


def _peer_gather(x3, h3, rows, gate, u_packed, v_packed, tb=128):
    t = x3.shape[0]
    diag, spread = _slab_maps()
    spread_t = spread.T
    rows1 = rows.reshape(-1)
    grid = (t // tb,)
    tok3 = pl.BlockSpec((tb, SUBLANES, LANES), lambda i: (i, 0, 0))
    tok2 = pl.BlockSpec((tb, PEER_SLOTS), lambda i: (i, 0))
    rep2 = pl.BlockSpec((tb, SLAB_COLS), lambda i: (i, 0))
    smem1 = pl.BlockSpec((tb * PEER_SLOTS,), lambda i: (i,), memory_space=pltpu.SMEM)
    table = pl.BlockSpec(u_packed.shape, lambda i: (0, 0), pipeline_mode=pl.Buffered(1))
    const = lambda a: pl.BlockSpec(a.shape, lambda i: (0, 0))
    slab = pltpu.VMEM((SLAB_ROWS, LANES), jnp.uint32)
    crep = pl.pallas_call(
        _peer_u_kernel,
        grid=grid,
        in_specs=[smem1, tok3, tok2, table, const(diag), const(spread), const(spread_t)],
        out_specs=rep2,
        out_shape=jax.ShapeDtypeStruct((t, SLAB_COLS), F32),
        scratch_shapes=[slab, slab, pltpu.VMEM((tb, SLAB_COLS), F32)],
        compiler_params=_params("arbitrary"),
        name="peer_u",
    )(rows1, h3, gate, u_packed, diag, spread, spread_t)
    return pl.pallas_call(
        _peer_v_kernel,
        grid=grid,
        in_specs=[smem1, rep2, tok3, table, const(diag)],
        out_specs=tok3,
        out_shape=jax.ShapeDtypeStruct((t, SUBLANES, LANES), F32),
        scratch_shapes=[slab, slab],
        compiler_params=_params("arbitrary"),
        name="peer_v",
    )(rows1, crep, x3, v_packed, diag)


def _peer_layer(x, g, w_query, sub_keys, expert_u, expert_v):
    t, d = x.shape
    h, rows, gate = _peer_route(x, g, w_query, sub_keys)
    x3 = x.reshape(t, SUBLANES, LANES)
    h3 = h.reshape(t, SUBLANES, LANES)
    y3 = _peer_gather(x3, h3, rows, gate, _pack_table(expert_u), _pack_table(expert_v))
    return y3.reshape(t, d)


def _hgrn_lower_bound(lb_logits, layer):
    p = jax.nn.softmax(lb_logits.astype(F32), axis=0)
    return jnp.cumsum(p, axis=0)[layer] - p[0]


def kernel(x, mix_norm_g, ffn_norm_g, final_norm_g, ssd_w_in, ssd_conv_w, ssd_conv_b, ssd_dt_bias, ssd_a_log, ssd_d_skip, ssd_norm_g, ssd_w_out, hgrn_w_in, hgrn_lb_logits, hgrn_norm_g, hgrn_w_out, peer_w_query, peer_sub_keys, peer_u, peer_v):
    bsz, s, d = x.shape
    depth = mix_norm_g.shape[0]
    xf = x.reshape(bsz * s, d)
    for layer in range(depth):
        j = layer // 2
        if layer % 2 == 0:
            xf = _ssd_layer(xf, bsz, mix_norm_g[layer], ssd_w_in[j], ssd_conv_w[j], ssd_conv_b[j],
                            ssd_dt_bias[j], ssd_a_log[j], ssd_d_skip[j], ssd_norm_g[j], ssd_w_out[j])
        else:
            lb = _hgrn_lower_bound(hgrn_lb_logits, layer)
            xf = _hgrn_layer(xf, bsz, mix_norm_g[layer], hgrn_w_in[j], lb, hgrn_norm_g[j], hgrn_w_out[j])
        xf = _peer_layer(xf, ffn_norm_g[layer], peer_w_query[layer], peer_sub_keys[layer],
                         peer_u[layer], peer_v[layer])
    return _final_norm(xf, final_norm_g).reshape(bsz, s, d)
```

```python
import functools

import jax
import jax.numpy as jnp
from jax import lax
from jax.experimental import pallas as pl
from jax.experimental.pallas import tpu as pltpu

F32 = jnp.float32
BF16 = jnp.bfloat16
HIGHEST = lax.Precision.HIGHEST

LANES = 128
SUBLANES = 8
VMEM_LIMIT = 56 * 1024 * 1024

D_MODEL = 1024
NORM_EPS = 1e-6

SSD_D_INNER = 2048
SSD_HEAD_DIM = 64
SSD_N_HEADS = 32
SSD_N_GROUPS = 4
SSD_HEADS_PER_GROUP = 8
SSD_D_STATE = 128
SSD_CONV_WIDTH = 4
SSD_CONV_DIM = SSD_D_INNER + 2 * SSD_N_GROUPS * SSD_D_STATE
SSD_GROUP_DIM = SSD_D_INNER // SSD_N_GROUPS
CHUNK = 128

HGRN_N_HEADS = 8
HGRN_HEAD_DIM = 128

PEER_N_KEYS = 128
PEER_N_HEADS = 8
PEER_D_HALF = 128
PEER_TOPK = 16
PEER_SLOTS = PEER_N_HEADS * PEER_TOPK
ROW_WORDS = D_MODEL // 2
ROW_SUB = ROW_WORDS // LANES


def _rmsnorm(x, g):
    r = lax.rsqrt(jnp.mean(x * x, axis=-1, keepdims=True) + NORM_EPS)
    return (x * r) * g


def _silu(x):
    return x * (1.0 / (1.0 + jnp.exp(-x)))


def _sigmoid(x):
    return 1.0 / (1.0 + jnp.exp(-x))


def _params(*sem):
    return pltpu.CompilerParams(dimension_semantics=sem, vmem_limit_bytes=VMEM_LIMIT)


def _norm_matmul_kernel(x_ref, g_ref, w_ref, o_ref, h_scr):
    @pl.when(pl.program_id(1) == 0)
    def _():
        h_scr[...] = _rmsnorm(x_ref[...], g_ref[...]).astype(BF16)

    o_ref[...] = jnp.dot(h_scr[...], w_ref[...], preferred_element_type=F32).astype(o_ref.dtype)


def _norm_matmul(x, g, w, tm=1024, tn=1024, out_dtype=BF16):
    t, d = x.shape
    n = w.shape[1]
    return pl.pallas_call(
        _norm_matmul_kernel,
        grid=(t // tm, n // tn),
        in_specs=[pl.BlockSpec((tm, d), lambda i, j: (i, 0)),
                  pl.BlockSpec((1, d), lambda i, j: (0, 0)),
                  pl.BlockSpec((d, tn), lambda i, j: (0, j))],
        out_specs=pl.BlockSpec((tm, tn), lambda i, j: (i, j)),
        out_shape=jax.ShapeDtypeStruct((t, n), out_dtype),
        scratch_shapes=[pltpu.VMEM((tm, d), BF16)],
        compiler_params=_params("parallel", "arbitrary"),
        name="norm_matmul",
    )(x, g.reshape(1, d), w)


def _matmul_residual_kernel(x_ref, a_ref, w_ref, o_ref):
    o_ref[...] = x_ref[...] + jnp.dot(a_ref[...], w_ref[...], preferred_element_type=F32)


def _matmul_residual(x, a, w, tm=512):
    t, d = x.shape
    k = a.shape[1]
    return pl.pallas_call(
        _matmul_residual_kernel,
        grid=(t // tm,),
        in_specs=[pl.BlockSpec((tm, d), lambda i: (i, 0)),
                  pl.BlockSpec((tm, k), lambda i: (i, 0)),
                  pl.BlockSpec((k, d), lambda i: (0, 0))],
        out_specs=pl.BlockSpec((tm, d), lambda i: (i, 0)),
        out_shape=jax.ShapeDtypeStruct((t, d), F32),
        compiler_params=_params("parallel"),
        name="matmul_residual",
    )(x, a, w)


def _final_norm_kernel(x_ref, g_ref, o_ref):
    o_ref[...] = _rmsnorm(x_ref[...], g_ref[...])


def _final_norm(x, g, tm=1024):
    t, d = x.shape
    return pl.pallas_call(
        _final_norm_kernel,
        grid=(t // tm,),
        in_specs=[pl.BlockSpec((tm, d), lambda i: (i, 0)),
                  pl.BlockSpec((1, d), lambda i: (0, 0))],
        out_specs=pl.BlockSpec((tm, d), lambda i: (i, 0)),
        out_shape=jax.ShapeDtypeStruct((t, d), F32),
        compiler_params=_params("parallel"),
        name="final_norm",
    )(x, g.reshape(1, d))


def _ssd_kernel(x_ref, zx_ref, gmix_ref, wdt_ref, dtb_ref, alog_ref, convw_ref, convb_ref,
                dskip_ref, expand_ref, gnorm_ref, wout_ref, o_ref, state_scr, tail_scr):
    L = CHUNK
    G, N, P = SSD_N_GROUPS, SSD_D_STATE, SSD_HEAD_DIM
    GD = SSD_GROUP_DIM

    @pl.when(pl.program_id(1) == 0)
    def _():
        state_scr[...] = jnp.zeros_like(state_scr)
        tail_scr[...] = jnp.zeros_like(tail_scr)

    x = x_ref[...]
    h = _rmsnorm(x, gmix_ref[...]).astype(BF16)
    dt_raw = jnp.dot(h, wdt_ref[...], preferred_element_type=F32) + dtb_ref[...]
    dt = jnp.maximum(dt_raw, 0.0) + jnp.log(1.0 + jnp.exp(-jnp.abs(dt_raw)))
    a = -jnp.exp(alog_ref[...])
    row = lax.broadcasted_iota(jnp.int32, (L, L), 0)
    col = lax.broadcasted_iota(jnp.int32, (L, L), 1)
    causal = row >= col
    tril = causal.astype(F32)
    acs = jnp.dot(tril, dt * a, preferred_element_type=F32, precision=HIGHEST)
    acs_t = acs.T
    dt_t = dt.T
    exp_acs = jnp.exp(acs)
    tail = jnp.exp(acs[L - 1:L, :] - acs) * dt
    both = jnp.concatenate([exp_acs, tail], axis=0)
    both_e = jnp.dot(both, expand_ref[...], preferred_element_type=F32, precision=HIGHEST)
    exp_acs_e = both_e[:L]
    tail_e = both_e[L:]

    zx = zx_ref[...]
    z = zx[:, :SSD_D_INNER].astype(F32)
    xbc = zx[:, SSD_D_INNER:].astype(F32)
    ext = jnp.concatenate([tail_scr[...], xbc], axis=0)
    tail_scr[...] = xbc[L - SUBLANES:, :]
    conv = convb_ref[...] + convw_ref[SSD_CONV_WIDTH - 1:SSD_CONV_WIDTH, :] * xbc
    for k in range(SSD_CONV_WIDTH - 1):
        off = SUBLANES - (SSD_CONV_WIDTH - 1) + k
        conv = conv + convw_ref[k:k + 1, :] * ext[off:off + L, :]
    xbc = _silu(conv)
    xs = xbc[:, :SSD_D_INNER]
    bm = xbc[:, SSD_D_INNER:SSD_D_INNER + G * N]
    cm = xbc[:, SSD_D_INNER + G * N:]

    xs_tail = (xs * tail_e).astype(BF16)
    xs_b = xs.astype(BF16)
    y_parts = []
    for g in range(G):
        b_g = bm[:, g * N:(g + 1) * N].astype(BF16)
        c_g = cm[:, g * N:(g + 1) * N].astype(BF16)
        cb = lax.dot_general(c_g, b_g, (((1,), (1,)), ((), ())), preferred_element_type=F32)
        st = state_scr[g]
        y_inter = jnp.dot(c_g, st.astype(BF16), preferred_element_type=F32)
        y_g = y_inter * exp_acs_e[:, g * GD:(g + 1) * GD]
        intra = []
        for j in range(SSD_HEADS_PER_GROUP):
            hh = g * SSD_HEADS_PER_GROUP + j
            seg = acs[:, hh:hh + 1] - acs_t[hh:hh + 1, :]
            decay = jnp.exp(jnp.where(causal, seg, -jnp.inf))
            w = (cb * decay * dt_t[hh:hh + 1, :]).astype(BF16)
            intra.append(jnp.dot(w, xs_b[:, hh * P:(hh + 1) * P], preferred_element_type=F32))
        y_g = y_g + jnp.concatenate(intra, axis=1)
        y_parts.append(y_g)
        upd = lax.dot_general(b_g, xs_tail[:, g * GD:(g + 1) * GD], (((0,), (0,)), ((), ())),
                              preferred_element_type=F32)
        state_scr[g] = st * exp_acs_e[L - 1:L, g * GD:(g + 1) * GD] + upd
    y = jnp.concatenate(y_parts, axis=1)
    y = y + xs * dskip_ref[...]
    y = y * _silu(z)
    normed = []
    for g in range(G):
        y_g = y[:, g * GD:(g + 1) * GD]
        normed.append(_rmsnorm(y_g, gnorm_ref[:, g * GD:(g + 1) * GD]))
    yn = jnp.concatenate(normed, axis=1).astype(BF16)
    o_ref[...] = x + jnp.dot(yn, wout_ref[...], preferred_element_type=F32)


def _pad_lanes(v):
    return jnp.pad(v.astype(F32), (0, LANES - v.shape[0])).reshape(1, LANES)


def _ssd_layer(x, bsz, g_mix, w_in, conv_w, conv_b, dt_bias, a_log, d_skip, norm_g, w_out):
    t, d = x.shape
    nc = t // bsz // CHUNK
    w_zx = w_in[:, :SSD_D_INNER + SSD_CONV_DIM].astype(BF16)
    w_dt = jnp.pad(w_in[:, SSD_D_INNER + SSD_CONV_DIM:], ((0, 0), (0, LANES - SSD_N_HEADS))).astype(BF16)
    zx = _norm_matmul(x, g_mix, w_zx)
    head_of_lane = jnp.arange(SSD_D_INNER) // SSD_HEAD_DIM
    expand = (jnp.arange(LANES)[:, None] == head_of_lane[None, :]).astype(F32)
    dskip_e = jnp.repeat(d_skip.astype(F32), SSD_HEAD_DIM).reshape(1, SSD_D_INNER)
    const = lambda shape: pl.BlockSpec(shape, lambda b, c: (0,) * len(shape))
    tok = lambda width: pl.BlockSpec((CHUNK, width), lambda b, c: (b * nc + c, 0))
    return pl.pallas_call(
        _ssd_kernel,
        grid=(bsz, nc),
        in_specs=[tok(d), tok(SSD_D_INNER + SSD_CONV_DIM), const((1, d)), const((d, LANES)),
                  const((1, LANES)), const((1, LANES)), const((SSD_CONV_WIDTH, SSD_CONV_DIM)),
                  const((1, SSD_CONV_DIM)), const((1, SSD_D_INNER)), const((LANES, SSD_D_INNER)),
                  const((1, SSD_D_INNER)), const((SSD_D_INNER, d))],
        out_specs=tok(d),
        out_shape=jax.ShapeDtypeStruct((t, d), F32),
        scratch_shapes=[pltpu.VMEM((SSD_N_GROUPS, SSD_D_STATE, SSD_GROUP_DIM), F32),
                        pltpu.VMEM((SUBLANES, SSD_CONV_DIM), F32)],
        compiler_params=_params("parallel", "arbitrary"),
        name="ssd_scan",
    )(x, zx, g_mix.reshape(1, d), w_dt, _pad_lanes(dt_bias), _pad_lanes(a_log), conv_w.astype(F32),
      conv_b.reshape(1, -1).astype(F32), dskip_e, expand, norm_g.reshape(1, -1).astype(F32),
      w_out.astype(BF16))


HGRN_BLOCK = SUBLANES
HGRN_PAIRS = CHUNK * HGRN_BLOCK
HGRN_HEADS_PER_STEP = 8


def _split3(x):
    x1 = x.astype(BF16)
    r1 = x - x1.astype(F32)
    x2 = r1.astype(BF16)
    x3 = (r1 - x2.astype(F32)).astype(BF16)
    return x1, x2, x3


def _hgrn_kernel(q_ref, f_ref, i_ref, g_ref, lb_ref, gnorm_ref, rep_ref, tile_ref, dif_ref, fold_ref,
                 o_ref, state_scr):
    L, C = CHUNK, HGRN_BLOCK
    NB = L // C
    K = HGRN_HEAD_DIM
    H = state_scr.shape[0]
    W = H * K
    nt = (((1,), (1,)), ((), ()))

    @pl.when(pl.program_id(2) == 0)
    def _():
        state_scr[...] = jnp.zeros_like(state_scr)

    lb = lb_ref[...]
    fl = f_ref[...].astype(F32)
    logf = jnp.log(lb + (1.0 - lb) * _sigmoid(fl))
    kk = (1.0 - lb) * _sigmoid(-fl)
    q = q_ref[...].astype(F32)
    row = lax.broadcasted_iota(jnp.int32, (L, L), 0)
    col = lax.broadcasted_iota(jnp.int32, (L, L), 1)
    b = jnp.dot((row >= col).astype(F32), logf, preferred_element_type=F32, precision=HIGHEST)
    b3 = b.reshape(NB, C, W)
    b_end3 = jnp.broadcast_to(b3[:, C - 1:C, :], b3.shape)
    b_prev3 = jnp.concatenate([jnp.zeros((1, C, W), F32), b_end3[:NB - 1]], axis=0)
    b_end = b_end3.reshape(L, W)
    b_prev = b_prev3.reshape(L, W)
    q_dec = (q * jnp.exp(b - b_prev)).astype(BF16)
    k_dec = (kk * jnp.exp(b_end - b)).astype(BF16)
    blk_dec = jnp.exp(b_end - b_prev)

    v_t = i_ref[...].astype(F32).T
    lane_blk = lax.broadcasted_iota(jnp.int32, (K, L), 1) // C
    inter = []
    for h in range(H):
        cols = slice(h * K, (h + 1) * K)
        v_h, k_h, q_h = v_t[cols], k_dec[:, cols], q_dec[:, cols]
        outer = [jnp.dot(jnp.where(lane_blk == i, v_h, 0.0).astype(BF16), k_h, preferred_element_type=F32)
                 for i in range(NB)]
        s = state_scr[h]
        o_h = jnp.zeros((K, L), F32)
        for i in range(NB):
            r = lax.dot_general(s.astype(BF16), q_h, nt, preferred_element_type=F32)
            o_h = jnp.where(lane_blk == i, r, o_h)
            s = s * blk_dec[i * C:i * C + 1, cols] + outer[i]
        state_scr[h] = s
        inter.append(o_h)

    q_e = jnp.dot(q.T.astype(BF16), rep_ref[...], preferred_element_type=F32)
    k_e = jnp.dot(kk.T.astype(BF16), tile_ref[...], preferred_element_type=F32)
    dif = dif_ref[...]
    seg = sum(jnp.dot(part, dif, preferred_element_type=F32) for part in _split3(b.T))
    j = lax.broadcasted_iota(jnp.int32, (W, HGRN_PAIRS), 1)
    ordered = (j % C) <= ((j // C) % C)
    prod = q_e * k_e * jnp.exp(jnp.where(ordered, seg, -jnp.inf))
    v_e = jnp.dot(v_t.astype(BF16), tile_ref[...], preferred_element_type=F32)
    weighted = [v_e[h * K:(h + 1) * K] * jnp.sum(prod[h * K:(h + 1) * K], axis=0, keepdims=True)
                for h in range(H)]
    o_t = jnp.concatenate(inter, axis=0) + jnp.dot(jnp.concatenate(weighted, axis=0).astype(BF16),
                                                   fold_ref[...], preferred_element_type=F32)
    o = o_t.T
    gate = _silu(g_ref[...].astype(F32))
    gnorm = gnorm_ref[...]
    for h in range(H):
        cols = slice(h * K, (h + 1) * K)
        o_ref[:, cols] = (_rmsnorm(o[:, cols], gnorm) * gate[:, cols]).astype(o_ref.dtype)


def _hgrn_pair_maps():
    c = HGRN_BLOCK
    j = jnp.arange(HGRN_PAIRS)
    t_of = j // c
    s_of = (t_of // c) * c + j % c
    t = jnp.arange(CHUNK)[:, None]
    rep = (t == t_of[None, :]).astype(F32)
    tile = (t == s_of[None, :]).astype(F32)
    return rep.astype(BF16), tile.astype(BF16), (rep - tile).astype(BF16), rep.T.astype(BF16)


def _hgrn_layer(x, bsz, g_mix, w_in, lower_bound, norm_g, w_out):
    t, d = x.shape
    nc = t // bsz // CHUNK
    hd = HGRN_HEAD_DIM
    qfig = _norm_matmul(x, g_mix, w_in.astype(BF16))
    hs = HGRN_HEADS_PER_STEP
    groups = HGRN_N_HEADS // hs
    part = lambda k: pl.BlockSpec((CHUNK, hs * hd), lambda b, h, c: (b * nc + c, k * groups + h))
    const = lambda shape: pl.BlockSpec(shape, lambda b, h, c: (0, 0))
    rep, tile, dif, fold = _hgrn_pair_maps()
    o = pl.pallas_call(
        _hgrn_kernel,
        grid=(bsz, groups, nc),
        in_specs=[part(0), part(1), part(2), part(3),
                  pl.BlockSpec((1, hs * hd), lambda b, h, c: (0, h)), const((1, hd)),
                  const(rep.shape), const(tile.shape), const(dif.shape), const(fold.shape)],
        out_specs=pl.BlockSpec((CHUNK, hs * hd), lambda b, h, c: (b * nc + c, h)),
        out_shape=jax.ShapeDtypeStruct((t, d), BF16),
        scratch_shapes=[pltpu.VMEM((hs, hd, hd), F32)],
        compiler_params=_params("parallel", "parallel", "arbitrary"),
        name="hgrn_scan",
    )(qfig, qfig, qfig, qfig, lower_bound.reshape(1, d).astype(F32), norm_g.reshape(1, hd).astype(F32),
      rep, tile, dif, fold)
    return _matmul_residual(x, o, w_out.astype(BF16))


def _topk_rows(s, k, payload=None):
    r = s.shape[0]
    rows = lax.broadcasted_iota(jnp.int32, s.shape, 0).astype(F32)
    vals, picks = [], []
    for _ in range(k):
        m = jnp.max(s, axis=0, keepdims=True)
        at_max = jnp.where(s == m, rows, float(r))
        pos = jnp.min(at_max, axis=0, keepdims=True)
        hit = at_max == pos
        vals.append(m)
        picks.append(pos if payload is None else jnp.sum(jnp.where(hit, payload, 0.0), axis=0, keepdims=True))
        s = jnp.where(hit, -jnp.inf, s)
    return jnp.concatenate(vals, axis=0), jnp.concatenate(picks, axis=0)


def _product_candidates(first, second, combine):
    half = PEER_TOPK // 2
    parts = [combine(first[0:1, :], second[0:half, :]), combine(first[0:1, :], second[half:, :])]
    parts += [combine(first[a:a + 1, :], second[0:half, :]) for a in range(1, half)]
    parts.append(combine(first[half:, :], second[0:1, :]))
    return jnp.concatenate(parts, axis=0)


def _peer_route_kernel(x_ref, g_ref, wq_ref, keys_ref, h_ref, idx_ref, gate_ref):
    h = _rmsnorm(x_ref[...], g_ref[...])
    h_ref[...] = h
    q_t = lax.dot_general(wq_ref[...], h.astype(BF16), (((1,), (1,)), ((), ())),
                          preferred_element_type=F32)
    idx_rows, gate_rows = [], []
    for head in range(PEER_N_HEADS):
        top_s, top_i = [], []
        for c in range(2):
            hc = head * 2 + c
            q_hc = q_t[hc * PEER_D_HALF:(hc + 1) * PEER_D_HALF, :].astype(BF16)
            sc = jnp.dot(keys_ref[hc], q_hc, preferred_element_type=F32)
            s, i = _topk_rows(sc, PEER_TOPK)
            top_s.append(s)
            top_i.append(i)
        cand_s = _product_candidates(top_s[0], top_s[1], lambda a, b: a + b)
        cand_i = _product_candidates(top_i[0], top_i[1], lambda a, b: a * float(PEER_N_KEYS) + b)
        best_s, best_i = _topk_rows(cand_s, PEER_TOPK, payload=cand_i)
        e = jnp.exp(best_s - best_s[0:1, :])
        gate_rows.append(e / jnp.sum(e, axis=0, keepdims=True))
        idx_rows.append(best_i)
    idx_ref[...] = jnp.concatenate(idx_rows, axis=0).T.astype(jnp.int32) * ROW_SUB
    gate_ref[...] = jnp.concatenate(gate_rows, axis=0).T


def _peer_route(x, g, w_query, sub_keys, tn=128):
    t, d = x.shape
    nq = w_query.shape[1]
    keys = sub_keys.reshape(2 * PEER_N_HEADS, PEER_N_KEYS, PEER_D_HALF).astype(BF16)
    return pl.pallas_call(
        _peer_route_kernel,
        grid=(t // tn,),
        in_specs=[pl.BlockSpec((tn, d), lambda i: (i, 0)),
                  pl.BlockSpec((1, d), lambda i: (0, 0)),
                  pl.BlockSpec((nq, d), lambda i: (0, 0)),
                  pl.BlockSpec(keys.shape, lambda i: (0, 0, 0))],
        out_specs=[pl.BlockSpec((tn, d), lambda i: (i, 0)),
                   pl.BlockSpec((tn, PEER_SLOTS), lambda i: (i, 0)),
                   pl.BlockSpec((tn, PEER_SLOTS), lambda i: (i, 0))],
        out_shape=[jax.ShapeDtypeStruct((t, d), F32),
                   jax.ShapeDtypeStruct((t, PEER_SLOTS), jnp.int32),
                   jax.ShapeDtypeStruct((t, PEER_SLOTS), F32)],
        compiler_params=_params("parallel"),
        name="peer_route",
    )(x, g.reshape(1, d), w_query.T.astype(BF16), keys)


SLAB_ROWS = PEER_SLOTS * ROW_SUB
SLAB_COLS = 2 * SLAB_ROWS
GATHER_UNROLL = 16


def _pack_table(t):
    b = lax.bitcast_convert_type(t.astype(BF16), jnp.uint16).astype(jnp.uint32)
    w = b[:, :ROW_WORDS] | (b[:, ROW_WORDS:] << 16)
    return w.reshape(-1, LANES)


def _slab_maps():
    col = jnp.arange(SLAB_COLS)
    row = jnp.arange(SUBLANES)[:, None]
    diag = (row == ((col // 2) % ROW_SUB + ROW_SUB * (col % 2))[None, :]).astype(F32)
    spread = (jnp.arange(PEER_SLOTS)[:, None] == (col // SUBLANES)[None, :]).astype(BF16)
    return diag, spread


def _gather_rows(idx_ref, tab_ref, slab, t):
    base = t * PEER_SLOTS
    for p in range(PEER_SLOTS):
        row = pl.multiple_of(idx_ref[base + p], ROW_SUB)
        slab[p * ROW_SUB:(p + 1) * ROW_SUB, :] = tab_ref[pl.ds(row, ROW_SUB), :]


def _pipelined_tokens(tb, gather, finish, slab_a, slab_b):
    gather(slab_a, 0)
    slabs = (slab_a, slab_b)

    def group(i, carry):
        t0 = GATHER_UNROLL * i
        for k in range(GATHER_UNROLL):
            gather(slabs[(k + 1) % 2], jnp.minimum(t0 + k + 1, tb - 1))
            finish(slabs[k % 2], t0 + k)
        return carry

    lax.fori_loop(0, tb // GATHER_UNROLL, group, 0)


_NT = (((1,), (1,)), ((), ()))


def _peer_u_kernel(idx_ref, h_ref, gate_ref, tab_ref, diag_ref, spread_ref, spread_t_ref, crep_ref,
                   slab_a, slab_b, rbuf):
    tb = h_ref.shape[0]
    diag = diag_ref[...]

    def finish(slab, t):
        w = pltpu.bitcast(slab[...], BF16)
        r = lax.dot_general(h_ref[t].astype(BF16), w, _NT, preferred_element_type=F32)
        rbuf[pl.ds(t, 1), :] = jnp.sum(r * diag, axis=0, keepdims=True)

    _pipelined_tokens(tb, functools.partial(_gather_rows, idx_ref, tab_ref), finish, slab_a, slab_b)
    spread_t = spread_t_ref[...]
    act = sum(jnp.dot(part, spread_t, preferred_element_type=F32) for part in _split3(rbuf[...]))
    gelu = 0.5 * act * (1.0 + lax.erf(act * (2.0 ** -0.5)))
    c = (gate_ref[...] * gelu).astype(BF16)
    crep_ref[...] = jnp.dot(c, spread_ref[...], preferred_element_type=F32)


def _peer_v_kernel(idx_ref, crep_ref, x_ref, tab_ref, diag_ref, o_ref, slab_a, slab_b):
    tb = x_ref.shape[0]
    diag = diag_ref[...]

    def finish(slab, t):
        w = pltpu.bitcast(slab[...], BF16)
        c = (crep_ref[pl.ds(t, 1), :] * diag).astype(BF16)
        y = jnp.dot(c, w, preferred_element_type=F32)
        flat = jnp.concatenate([y[s:s + 1, :] for s in range(SUBLANES)], axis=1)
        o_ref[pl.ds(t, 1), :] = x_ref[pl.ds(t, 1), :] + flat

    _pipelined_tokens(tb, functools.partial(_gather_rows, idx_ref, tab_ref), finish, slab_a, slab_b)


def _peer_gather(x, h, rows, gate, u_packed, v_packed, tb=128):
    t, d = x.shape
    diag, spread = _slab_maps()
    spread_t = spread.T
    rows1 = rows.reshape(-1)
    grid = (t // tb,)
    tok = pl.BlockSpec((tb, d), lambda i: (i, 0))
    tok3 = pl.BlockSpec((tb, SUBLANES, LANES), lambda i: (i, 0, 0))
    tok2 = pl.BlockSpec((tb, PEER_SLOTS), lambda i: (i, 0))
    rep2 = pl.BlockSpec((tb, SLAB_COLS), lambda i: (i, 0))
    smem1 = pl.BlockSpec((tb * PEER_SLOTS,), lambda i: (i,), memory_space=pltpu.SMEM)
    table = pl.BlockSpec(u_packed.shape, lambda i: (0, 0), pipeline_mode=pl.Buffered(1))
    const = lambda a: pl.BlockSpec(a.shape, lambda i: (0, 0))
    slab = pltpu.VMEM((SLAB_ROWS, LANES), jnp.uint32)
    crep = pl.pallas_call(
        _peer_u_kernel,
        grid=grid,
        in_specs=[smem1, tok3, tok2, table, const(diag), const(spread), const(spread_t)],
        out_specs=rep2,
        out_shape=jax.ShapeDtypeStruct((t, SLAB_COLS), F32),
        scratch_shapes=[slab, slab, pltpu.VMEM((tb, SLAB_COLS), F32)],
        compiler_params=_params("arbitrary"),
        name="peer_u",
    )(rows1, h.reshape(t, SUBLANES, LANES), gate, u_packed, diag, spread, spread_t)
    return pl.pallas_call(
        _peer_v_kernel,
        grid=grid,
        in_specs=[smem1, rep2, tok, table, const(diag)],
        out_specs=tok,
        out_shape=jax.ShapeDtypeStruct((t, d), F32),
        scratch_shapes=[slab, slab],
        compiler_params=_params("arbitrary"),
        name="peer_v",
    )(rows1, crep, x, v_packed, diag)


def _peer_layer(x, g, w_query, sub_keys, expert_u, expert_v):
    h, rows, gate = _peer_route(x, g, w_query, sub_keys)
    return _peer_gather(x, h, rows, gate, _pack_table(expert_u), _pack_table(expert_v))


def _hgrn_lower_bound(lb_logits, layer):
    p = jax.nn.softmax(lb_logits.astype(F32), axis=0)
    return jnp.cumsum(p, axis=0)[layer] - p[0]


def kernel(x, mix_norm_g, ffn_norm_g, final_norm_g, ssd_w_in, ssd_conv_w, ssd_conv_b, ssd_dt_bias, ssd_a_log, ssd_d_skip, ssd_norm_g, ssd_w_out, hgrn_w_in, hgrn_lb_logits, hgrn_norm_g, hgrn_w_out, peer_w_query, peer_sub_keys, peer_u, peer_v):
    bsz, s, d = x.shape
    depth = mix_norm_g.shape[0]
    xf = x.reshape(bsz * s, d)
    for layer in range(depth):
        j = layer // 2
        if layer % 2 == 0:
            xf = _ssd_layer(xf, bsz, mix_norm_g[layer], ssd_w_in[j], ssd_conv_w[j], ssd_conv_b[j],
                            ssd_dt_bias[j], ssd_a_log[j], ssd_d_skip[j], ssd_norm_g[j], ssd_w_out[j])
        else:
            lb = _hgrn_lower_bound(hgrn_lb_logits, layer)
            xf = _hgrn_layer(xf, bsz, mix_norm_g[layer], hgrn_w_in[j], lb, hgrn_norm_g[j], hgrn_w_out[j])
        xf = _peer_layer(xf, ffn_norm_g[layer], peer_w_query[layer], peer_sub_keys[layer],
                         peer_u[layer], peer_v[layer])
    return _final_norm(xf, final_norm_g).reshape(bsz, s, d)
```

```python
import functools

import jax
import jax.numpy as jnp
from jax import lax
from jax.experimental import pallas as pl
from jax.experimental.pallas import tpu as pltpu

F32 = jnp.float32
BF16 = jnp.bfloat16
HIGHEST = lax.Precision.HIGHEST

LANES = 128
SUBLANES = 8
VMEM_LIMIT = 56 * 1024 * 1024

D_MODEL = 1024
NORM_EPS = 1e-6

SSD_D_INNER = 2048
SSD_HEAD_DIM = 64
SSD_N_HEADS = 32
SSD_N_GROUPS = 4
SSD_HEADS_PER_GROUP = 8
SSD_D_STATE = 128
SSD_CONV_WIDTH = 4
SSD_CONV_DIM = SSD_D_INNER + 2 * SSD_N_GROUPS * SSD_D_STATE
SSD_GROUP_DIM = SSD_D_INNER // SSD_N_GROUPS
CHUNK = 128

HGRN_N_HEADS = 8
HGRN_HEAD_DIM = 128

PEER_N_KEYS = 128
PEER_N_HEADS = 8
PEER_D_HALF = 128
PEER_TOPK = 16
PEER_SLOTS = PEER_N_HEADS * PEER_TOPK
ROW_WORDS = D_MODEL // 2
ROW_SUB = ROW_WORDS // LANES


def _rmsnorm(x, g):
    r = lax.rsqrt(jnp.mean(x * x, axis=-1, keepdims=True) + NORM_EPS)
    return (x * r) * g


def _silu(x):
    return x * (1.0 / (1.0 + jnp.exp(-x)))


def _sigmoid(x):
    return 1.0 / (1.0 + jnp.exp(-x))


def _params(*sem):
    return pltpu.CompilerParams(dimension_semantics=sem, vmem_limit_bytes=VMEM_LIMIT)


def _norm_matmul_kernel(x_ref, g_ref, w_ref, o_ref, h_scr):
    @pl.when(pl.program_id(1) == 0)
    def _():
        h_scr[...] = _rmsnorm(x_ref[...], g_ref[...]).astype(BF16)

    o_ref[...] = jnp.dot(h_scr[...], w_ref[...], preferred_element_type=F32).astype(o_ref.dtype)


def _norm_matmul(x, g, w, tm=1024, tn=1024, out_dtype=BF16):
    t, d = x.shape
    n = w.shape[1]
    return pl.pallas_call(
        _norm_matmul_kernel,
        grid=(t // tm, n // tn),
        in_specs=[pl.BlockSpec((tm, d), lambda i, j: (i, 0)),
                  pl.BlockSpec((1, d), lambda i, j: (0, 0)),
                  pl.BlockSpec((d, tn), lambda i, j: (0, j))],
        out_specs=pl.BlockSpec((tm, tn), lambda i, j: (i, j)),
        out_shape=jax.ShapeDtypeStruct((t, n), out_dtype),
        scratch_shapes=[pltpu.VMEM((tm, d), BF16)],
        compiler_params=_params("parallel", "arbitrary"),
        name="norm_matmul",
    )(x, g.reshape(1, d), w)


def _matmul_residual_kernel(x_ref, a_ref, w_ref, o_ref):
    o_ref[...] = x_ref[...] + jnp.dot(a_ref[...], w_ref[...], preferred_element_type=F32)


def _matmul_residual(x, a, w, tm=512):
    t, d = x.shape
    k = a.shape[1]
    return pl.pallas_call(
        _matmul_residual_kernel,
        grid=(t // tm,),
        in_specs=[pl.BlockSpec((tm, d), lambda i: (i, 0)),
                  pl.BlockSpec((tm, k), lambda i: (i, 0)),
                  pl.BlockSpec((k, d), lambda i: (0, 0))],
        out_specs=pl.BlockSpec((tm, d), lambda i: (i, 0)),
        out_shape=jax.ShapeDtypeStruct((t, d), F32),
        compiler_params=_params("parallel"),
        name="matmul_residual",
    )(x, a, w)


def _final_norm_kernel(x_ref, g_ref, o_ref):
    o_ref[...] = _rmsnorm(x_ref[...], g_ref[...])


def _final_norm(x, g, tm=1024):
    t, d = x.shape
    return pl.pallas_call(
        _final_norm_kernel,
        grid=(t // tm,),
        in_specs=[pl.BlockSpec((tm, d), lambda i: (i, 0)),
                  pl.BlockSpec((1, d), lambda i: (0, 0))],
        out_specs=pl.BlockSpec((tm, d), lambda i: (i, 0)),
        out_shape=jax.ShapeDtypeStruct((t, d), F32),
        compiler_params=_params("parallel"),
        name="final_norm",
    )(x, g.reshape(1, d))


def _ssd_kernel(x_ref, zx_ref, gmix_ref, wdt_ref, dtb_ref, alog_ref, convw_ref, convb_ref,
                dskip_ref, expand_ref, gnorm_ref, wout_ref, o_ref, state_scr, tail_scr):
    L = CHUNK
    G, N, P = SSD_N_GROUPS, SSD_D_STATE, SSD_HEAD_DIM
    GD = SSD_GROUP_DIM

    @pl.when(pl.program_id(1) == 0)
    def _():
        state_scr[...] = jnp.zeros_like(state_scr)
        tail_scr[...] = jnp.zeros_like(tail_scr)

    x = x_ref[...]
    h = _rmsnorm(x, gmix_ref[...]).astype(BF16)
    dt_raw = jnp.dot(h, wdt_ref[...], preferred_element_type=F32) + dtb_ref[...]
    dt = jnp.maximum(dt_raw, 0.0) + jnp.log(1.0 + jnp.exp(-jnp.abs(dt_raw)))
    a = -jnp.exp(alog_ref[...])
    row = lax.broadcasted_iota(jnp.int32, (L, L), 0)
    col = lax.broadcasted_iota(jnp.int32, (L, L), 1)
    causal = row >= col
    tril = causal.astype(F32)
    acs = jnp.dot(tril, dt * a, preferred_element_type=F32, precision=HIGHEST)
    acs_t = acs.T
    dt_t = dt.T
    exp_acs = jnp.exp(acs)
    tail = jnp.exp(acs[L - 1:L, :] - acs) * dt
    both = jnp.concatenate([exp_acs, tail], axis=0)
    both_e = jnp.dot(both, expand_ref[...], preferred_element_type=F32, precision=HIGHEST)
    exp_acs_e = both_e[:L]
    tail_e = both_e[L:]

    zx = zx_ref[...]
    z = zx[:, :SSD_D_INNER].astype(F32)
    xbc = zx[:, SSD_D_INNER:].astype(F32)
    ext = jnp.concatenate([tail_scr[...], xbc], axis=0)
    tail_scr[...] = xbc[L - SUBLANES:, :]
    conv = convb_ref[...] + convw_ref[SSD_CONV_WIDTH - 1:SSD_CONV_WIDTH, :] * xbc
    for k in range(SSD_CONV_WIDTH - 1):
        off = SUBLANES - (SSD_CONV_WIDTH - 1) + k
        conv = conv + convw_ref[k:k + 1, :] * ext[off:off + L, :]
    xbc = _silu(conv)
    xs = xbc[:, :SSD_D_INNER]
    bm = xbc[:, SSD_D_INNER:SSD_D_INNER + G * N]
    cm = xbc[:, SSD_D_INNER + G * N:]

    xs_tail = (xs * tail_e).astype(BF16)
    xs_b = xs.astype(BF16)
    y_parts = []
    for g in range(G):
        b_g = bm[:, g * N:(g + 1) * N].astype(BF16)
        c_g = cm[:, g * N:(g + 1) * N].astype(BF16)
        cb = lax.dot_general(c_g, b_g, (((1,), (1,)), ((), ())), preferred_element_type=F32)
        st = state_scr[g]
        y_inter = jnp.dot(c_g, st.astype(BF16), preferred_element_type=F32)
        y_g = y_inter * exp_acs_e[:, g * GD:(g + 1) * GD]
        intra = []
        for j in range(SSD_HEADS_PER_GROUP):
            hh = g * SSD_HEADS_PER_GROUP + j
            seg = acs[:, hh:hh + 1] - acs_t[hh:hh + 1, :]
            decay = jnp.exp(jnp.where(causal, seg, -jnp.inf))
            w = (cb * decay * dt_t[hh:hh + 1, :]).astype(BF16)
            intra.append(jnp.dot(w, xs_b[:, hh * P:(hh + 1) * P], preferred_element_type=F32))
        y_g = y_g + jnp.concatenate(intra, axis=1)
        y_parts.append(y_g)
        upd = lax.dot_general(b_g, xs_tail[:, g * GD:(g + 1) * GD], (((0,), (0,)), ((), ())),
                              preferred_element_type=F32)
        state_scr[g] = st * exp_acs_e[L - 1:L, g * GD:(g + 1) * GD] + upd
    y = jnp.concatenate(y_parts, axis=1)
    y = y + xs * dskip_ref[...]
    y = y * _silu(z)
    normed = []
    for g in range(G):
        y_g = y[:, g * GD:(g + 1) * GD]
        normed.append(_rmsnorm(y_g, gnorm_ref[:, g * GD:(g + 1) * GD]))
    yn = jnp.concatenate(normed, axis=1).astype(BF16)
    o_ref[...] = x + jnp.dot(yn, wout_ref[...], preferred_element_type=F32)


def _pad_lanes(v):
    return jnp.pad(v.astype(F32), (0, LANES - v.shape[0])).reshape(1, LANES)


def _ssd_layer(x, bsz, g_mix, w_in, conv_w, conv_b, dt_bias, a_log, d_skip, norm_g, w_out):
    t, d = x.shape
    nc = t // bsz // CHUNK
    w_zx = w_in[:, :SSD_D_INNER + SSD_CONV_DIM].astype(BF16)
    w_dt = jnp.pad(w_in[:, SSD_D_INNER + SSD_CONV_DIM:], ((0, 0), (0, LANES - SSD_N_HEADS))).astype(BF16)
    zx = _norm_matmul(x, g_mix, w_zx)
    head_of_lane = jnp.arange(SSD_D_INNER) // SSD_HEAD_DIM
    expand = (jnp.arange(LANES)[:, None] == head_of_lane[None, :]).astype(F32)
    dskip_e = jnp.repeat(d_skip.astype(F32), SSD_HEAD_DIM).reshape(1, SSD_D_INNER)
    const = lambda shape: pl.BlockSpec(shape, lambda b, c: (0,) * len(shape))
    tok = lambda width: pl.BlockSpec((CHUNK, width), lambda b, c: (b * nc + c, 0))
    return pl.pallas_call(
        _ssd_kernel,
        grid=(bsz, nc),
        in_specs=[tok(d), tok(SSD_D_INNER + SSD_CONV_DIM), const((1, d)), const((d, LANES)),
                  const((1, LANES)), const((1, LANES)), const((SSD_CONV_WIDTH, SSD_CONV_DIM)),
                  const((1, SSD_CONV_DIM)), const((1, SSD_D_INNER)), const((LANES, SSD_D_INNER)),
                  const((1, SSD_D_INNER)), const((SSD_D_INNER, d))],
        out_specs=tok(d),
        out_shape=jax.ShapeDtypeStruct((t, d), F32),
        scratch_shapes=[pltpu.VMEM((SSD_N_GROUPS, SSD_D_STATE, SSD_GROUP_DIM), F32),
                        pltpu.VMEM((SUBLANES, SSD_CONV_DIM), F32)],
        compiler_params=_params("parallel", "arbitrary"),
        name="ssd_scan",
    )(x, zx, g_mix.reshape(1, d), w_dt, _pad_lanes(dt_bias), _pad_lanes(a_log), conv_w.astype(F32),
      conv_b.reshape(1, -1).astype(F32), dskip_e, expand, norm_g.reshape(1, -1).astype(F32),
      w_out.astype(BF16))


HGRN_BLOCK = SUBLANES
HGRN_PAIRS = CHUNK * HGRN_BLOCK
HGRN_HEADS_PER_STEP = 8


def _split3(x):
    x1 = x.astype(BF16)
    r1 = x - x1.astype(F32)
    x2 = r1.astype(BF16)
    x3 = (r1 - x2.astype(F32)).astype(BF16)
    return x1, x2, x3


def _hgrn_kernel(q_ref, f_ref, i_ref, g_ref, lb_ref, gnorm_ref, rep_ref, tile_ref, dif_ref, fold_ref,
                 o_ref, state_scr):
    L, C = CHUNK, HGRN_BLOCK
    NB = L // C
    K = HGRN_HEAD_DIM
    H = state_scr.shape[0]
    W = H * K
    nt = (((1,), (1,)), ((), ()))

    @pl.when(pl.program_id(2) == 0)
    def _():
        state_scr[...] = jnp.zeros_like(state_scr)

    lb = lb_ref[...]
    fl = f_ref[...].astype(F32)
    logf = jnp.log(lb + (1.0 - lb) * _sigmoid(fl))
    kk = (1.0 - lb) * _sigmoid(-fl)
    q = q_ref[...].astype(F32)
    row = lax.broadcasted_iota(jnp.int32, (L, L), 0)
    col = lax.broadcasted_iota(jnp.int32, (L, L), 1)
    b = jnp.dot((row >= col).astype(F32), logf, preferred_element_type=F32, precision=HIGHEST)
    b3 = b.reshape(NB, C, W)
    b_end3 = jnp.broadcast_to(b3[:, C - 1:C, :], b3.shape)
    b_prev3 = jnp.concatenate([jnp.zeros((1, C, W), F32), b_end3[:NB - 1]], axis=0)
    b_end = b_end3.reshape(L, W)
    b_prev = b_prev3.reshape(L, W)
    q_dec = (q * jnp.exp(b - b_prev)).astype(BF16)
    k_dec = (kk * jnp.exp(b_end - b)).astype(BF16)
    blk_dec = jnp.exp(b_end - b_prev)

    v_t = i_ref[...].astype(F32).T
    lane_blk = lax.broadcasted_iota(jnp.int32, (K, L), 1) // C
    inter = []
    for h in range(H):
        cols = slice(h * K, (h + 1) * K)
        v_h, k_h, q_h = v_t[cols], k_dec[:, cols], q_dec[:, cols]
        outer = [jnp.dot(jnp.where(lane_blk == i, v_h, 0.0).astype(BF16), k_h, preferred_element_type=F32)
                 for i in range(NB)]
        s = state_scr[h]
        o_h = jnp.zeros((K, L), F32)
        for i in range(NB):
            r = lax.dot_general(s.astype(BF16), q_h, nt, preferred_element_type=F32)
            o_h = jnp.where(lane_blk == i, r, o_h)
            s = s * blk_dec[i * C:i * C + 1, cols] + outer[i]
        state_scr[h] = s
        inter.append(o_h)

    q_e = jnp.dot(q.T.astype(BF16), rep_ref[...], preferred_element_type=F32)
    k_e = jnp.dot(kk.T.astype(BF16), tile_ref[...], preferred_element_type=F32)
    dif = dif_ref[...]
    seg = sum(jnp.dot(part, dif, preferred_element_type=F32) for part in _split3(b.T))
    j = lax.broadcasted_iota(jnp.int32, (W, HGRN_PAIRS), 1)
    ordered = (j % C) <= ((j // C) % C)
    prod = q_e * k_e * jnp.exp(jnp.where(ordered, seg, -jnp.inf))
    v_e = jnp.dot(v_t.astype(BF16), tile_ref[...], preferred_element_type=F32)
    weighted = [v_e[h * K:(h + 1) * K] * jnp.sum(prod[h * K:(h + 1) * K], axis=0, keepdims=True)
                for h in range(H)]
    o_t = jnp.concatenate(inter, axis=0) + jnp.dot(jnp.concatenate(weighted, axis=0).astype(BF16),
                                                   fold_ref[...], preferred_element_type=F32)
    o = o_t.T
    gate = _silu(g_ref[...].astype(F32))
    gnorm = gnorm_ref[...]
    for h in range(H):
        cols = slice(h * K, (h + 1) * K)
        o_ref[:, cols] = (_rmsnorm(o[:, cols], gnorm) * gate[:, cols]).astype(o_ref.dtype)


def _hgrn_pair_maps():
    c = HGRN_BLOCK
    j = jnp.arange(HGRN_PAIRS)
    t_of = j // c
    s_of = (t_of // c) * c + j % c
    t = jnp.arange(CHUNK)[:, None]
    rep = (t == t_of[None, :]).astype(F32)
    tile = (t == s_of[None, :]).astype(F32)
    return rep.astype(BF16), tile.astype(BF16), (rep - tile).astype(BF16), rep.T.astype(BF16)


def _hgrn_layer(x, bsz, g_mix, w_in, lower_bound, norm_g, w_out):
    t, d = x.shape
    nc = t // bsz // CHUNK
    hd = HGRN_HEAD_DIM
    qfig = _norm_matmul(x, g_mix, w_in.astype(BF16))
    hs = HGRN_HEADS_PER_STEP
    groups = HGRN_N_HEADS // hs
    part = lambda k: pl.BlockSpec((CHUNK, hs * hd), lambda b, h, c: (b * nc + c, k * groups + h))
    const = lambda shape: pl.BlockSpec(shape, lambda b, h, c: (0, 0))
    rep, tile, dif, fold = _hgrn_pair_maps()
    o = pl.pallas_call(
        _hgrn_kernel,
        grid=(bsz, groups, nc),
        in_specs=[part(0), part(1), part(2), part(3),
                  pl.BlockSpec((1, hs * hd), lambda b, h, c: (0, h)), const((1, hd)),
                  const(rep.shape), const(tile.shape), const(dif.shape), const(fold.shape)],
        out_specs=pl.BlockSpec((CHUNK, hs * hd), lambda b, h, c: (b * nc + c, h)),
        out_shape=jax.ShapeDtypeStruct((t, d), BF16),
        scratch_shapes=[pltpu.VMEM((hs, hd, hd), F32)],
        compiler_params=_params("parallel", "parallel", "arbitrary"),
        name="hgrn_scan",
    )(qfig, qfig, qfig, qfig, lower_bound.reshape(1, d).astype(F32), norm_g.reshape(1, hd).astype(F32),
      rep, tile, dif, fold)
    return _matmul_residual(x, o, w_out.astype(BF16))


def _drain(steps):
    try:
        while True:
            next(steps)
    except StopIteration as done:
        return done.value


def _topk_rows_steps(s, k, payload=None):
    r = s.shape[0]
    rows = lax.broadcasted_iota(jnp.int32, s.shape, 0).astype(F32)
    vals, picks = [], []
    for _ in range(k):
        m = jnp.max(s, axis=0, keepdims=True)
        at_max = jnp.where(s == m, rows, float(r))
        pos = jnp.min(at_max, axis=0, keepdims=True)
        hit = at_max == pos
        vals.append(m)
        picks.append(pos if payload is None else jnp.sum(jnp.where(hit, payload, 0.0), axis=0, keepdims=True))
        s = jnp.where(hit, -jnp.inf, s)
        yield
    return jnp.concatenate(vals, axis=0), jnp.concatenate(picks, axis=0)


def _topk_rows(s, k, payload=None):
    return _drain(_topk_rows_steps(s, k, payload))


def _route_head_steps(keys_ref, qt_ref, idx_scr, gate_scr, head):
    top_s, top_i = [], []
    for c in range(2):
        hc = head * 2 + c
        q_hc = qt_ref[pl.ds(pl.multiple_of(hc * PEER_D_HALF, PEER_D_HALF), PEER_D_HALF), :]
        sc = jnp.dot(keys_ref[hc], q_hc, preferred_element_type=F32)
        s, i = yield from _topk_rows_steps(sc, PEER_TOPK)
        top_s.append(s)
        top_i.append(i)
    cand_s = _product_candidates(top_s[0], top_s[1], lambda a, b: a + b)
    cand_i = _product_candidates(top_i[0], top_i[1], lambda a, b: a * float(PEER_N_KEYS) + b)
    best_s, best_i = yield from _topk_rows_steps(cand_s, PEER_TOPK, payload=cand_i)
    e = jnp.exp(best_s - best_s[0:1, :])
    out_rows = pl.ds(pl.multiple_of(head * PEER_TOPK, PEER_TOPK), PEER_TOPK)
    idx_scr[out_rows, :] = best_i
    gate_scr[out_rows, :] = e / jnp.sum(e, axis=0, keepdims=True)


def _product_candidates(first, second, combine):
    half = PEER_TOPK // 2
    parts = [combine(first[0:1, :], second[0:half, :]), combine(first[0:1, :], second[half:, :])]
    parts += [combine(first[a:a + 1, :], second[0:half, :]) for a in range(1, half)]
    parts.append(combine(first[half:, :], second[0:1, :]))
    return jnp.concatenate(parts, axis=0)


def _peer_route_kernel(x_ref, g_ref, wq_ref, keys_ref, h_ref, idx_ref, gate_ref):
    h = _rmsnorm(x_ref[...], g_ref[...])
    h_ref[...] = h
    q_t = lax.dot_general(wq_ref[...], h.astype(BF16), (((1,), (1,)), ((), ())),
                          preferred_element_type=F32)
    idx_rows, gate_rows = [], []
    for head in range(PEER_N_HEADS):
        top_s, top_i = [], []
        for c in range(2):
            hc = head * 2 + c
            q_hc = q_t[hc * PEER_D_HALF:(hc + 1) * PEER_D_HALF, :].astype(BF16)
            sc = jnp.dot(keys_ref[hc], q_hc, preferred_element_type=F32)
            s, i = _topk_rows(sc, PEER_TOPK)
            top_s.append(s)
            top_i.append(i)
        cand_s = _product_candidates(top_s[0], top_s[1], lambda a, b: a + b)
        cand_i = _product_candidates(top_i[0], top_i[1], lambda a, b: a * float(PEER_N_KEYS) + b)
        best_s, best_i = _topk_rows(cand_s, PEER_TOPK, payload=cand_i)
        e = jnp.exp(best_s - best_s[0:1, :])
        gate_rows.append(e / jnp.sum(e, axis=0, keepdims=True))
        idx_rows.append(best_i)
    idx_ref[...] = jnp.concatenate(idx_rows, axis=0).T.astype(jnp.int32) * ROW_SUB
    gate_ref[...] = jnp.concatenate(gate_rows, axis=0).T


def _peer_route(x, g, wq_t, keys, n_tokens, tn=128):
    t, d = x.shape
    t = n_tokens
    nq = wq_t.shape[0]
    return pl.pallas_call(
        _peer_route_kernel,
        grid=(t // tn,),
        in_specs=[pl.BlockSpec((tn, d), lambda i: (i, 0)),
                  pl.BlockSpec((1, d), lambda i: (0, 0)),
                  pl.BlockSpec((nq, d), lambda i: (0, 0)),
                  pl.BlockSpec(keys.shape, lambda i: (0, 0, 0))],
        out_specs=[pl.BlockSpec((tn, d), lambda i: (i, 0)),
                   pl.BlockSpec((tn, PEER_SLOTS), lambda i: (i, 0)),
                   pl.BlockSpec((tn, PEER_SLOTS), lambda i: (i, 0))],
        out_shape=[jax.ShapeDtypeStruct((t, d), F32),
                   jax.ShapeDtypeStruct((t, PEER_SLOTS), jnp.int32),
                   jax.ShapeDtypeStruct((t, PEER_SLOTS), F32)],
        compiler_params=_params("parallel"),
        name="peer_route",
    )(x, g.reshape(1, d), wq_t, keys)


SLAB_ROWS = PEER_SLOTS * ROW_SUB
SLAB_COLS = 2 * SLAB_ROWS
GATHER_UNROLL = 16


def _pack_table(t):
    b = lax.bitcast_convert_type(t.astype(BF16), jnp.uint16).astype(jnp.uint32)
    w = b[:, :ROW_WORDS] | (b[:, ROW_WORDS:] << 16)
    return w.reshape(-1, LANES)


def _slab_maps():
    col = jnp.arange(SLAB_COLS)
    row = jnp.arange(SUBLANES)[:, None]
    diag = (row == ((col // 2) % ROW_SUB + ROW_SUB * (col % 2))[None, :]).astype(F32)
    spread = (jnp.arange(PEER_SLOTS)[:, None] == (col // SUBLANES)[None, :]).astype(BF16)
    return diag, spread


SIDE_EVERY = 32


def _gather_rows(idx_ref, tab_ref, slab, t, side=None):
    base = t * PEER_SLOTS
    for p in range(PEER_SLOTS):
        row = pl.multiple_of(idx_ref[base + p], ROW_SUB)
        slab[p * ROW_SUB:(p + 1) * ROW_SUB, :] = tab_ref[pl.ds(row, ROW_SUB), :]
        if side is not None and p % SIDE_EVERY == SIDE_EVERY - 1:
            next(side, None)


def _pipelined_tokens(tb, gather, finish, slab_a, slab_b, unroll=GATHER_UNROLL, side_work=None):
    gather(slab_a, 0)
    slabs = (slab_a, slab_b)

    def group(i, carry):
        t0 = unroll * i
        side = None if side_work is None else side_work(i)
        for k in range(unroll):
            gather(slabs[(k + 1) % 2], jnp.minimum(t0 + k + 1, tb - 1), side)
            finish(slabs[k % 2], t0 + k)
        if side is not None:
            _drain(side)
        return carry

    lax.fori_loop(0, tb // unroll, group, 0)


_NT = (((1,), (1,)), ((), ()))


def _peer_u_kernel(idx_ref, h_ref, gate_ref, tab_ref, diag_ref, spread_ref, spread_t_ref, crep_ref,
                   slab_a, slab_b, rbuf, side_work=None):
    tb = h_ref.shape[0]
    diag = diag_ref[...]

    def finish(slab, t):
        w = pltpu.bitcast(slab[...], BF16)
        r = lax.dot_general(h_ref[t].astype(BF16), w, _NT, preferred_element_type=F32)
        rbuf[pl.ds(t, 1), :] = jnp.sum(r * diag, axis=0, keepdims=True)

    _pipelined_tokens(tb, functools.partial(_gather_rows, idx_ref, tab_ref), finish, slab_a, slab_b,
                      side_work=side_work)
    spread_t = spread_t_ref[...]
    act = sum(jnp.dot(part, spread_t, preferred_element_type=F32) for part in _split3(rbuf[...]))
    gelu = 0.5 * act * (1.0 + lax.erf(act * (2.0 ** -0.5)))
    c = (gate_ref[...] * gelu).astype(BF16)
    crep_ref[...] = jnp.dot(c, spread_ref[...], preferred_element_type=F32)


def _peer_u_route_kernel(idx_ref, h_ref, gate_ref, tab_ref, diag_ref, spread_ref, spread_t_ref,
                         x2_ref, g_ref, wq_ref, keys_ref, crep_ref, h2_ref, idx2_ref, gate2_ref,
                         slab_a, slab_b, rbuf, qt_scr, idx_scr, gate_scr):
    h2 = _rmsnorm(x2_ref[...], g_ref[...])
    h2_ref[...] = h2
    qt_scr[...] = lax.dot_general(wq_ref[...], h2.astype(BF16), _NT, preferred_element_type=F32).astype(BF16)
    _peer_u_kernel(idx_ref, h_ref, gate_ref, tab_ref, diag_ref, spread_ref, spread_t_ref, crep_ref,
                   slab_a, slab_b, rbuf,
                   side_work=functools.partial(_route_head_steps, keys_ref, qt_scr, idx_scr, gate_scr))
    idx2_ref[...] = idx_scr[...].T.astype(jnp.int32) * ROW_SUB
    gate2_ref[...] = gate_scr[...].T


def _peer_v_kernel(idx_ref, crep_a_ref, crep_b_ref, x_ref, tab_ref, diag_ref, o_ref, slab_a, slab_b, *,
                   blocks_a):
    tb = x_ref.shape[0]
    diag = diag_ref[...]
    in_a = pl.program_id(0) < blocks_a

    def finish(slab, t):
        w = pltpu.bitcast(slab[...], BF16)
        crep = jnp.where(in_a, crep_a_ref[pl.ds(t, 1), :], crep_b_ref[pl.ds(t, 1), :])
        c = (crep * diag).astype(BF16)
        y = jnp.dot(c, w, preferred_element_type=F32)
        flat = jnp.concatenate([y[s:s + 1, :] for s in range(SUBLANES)], axis=1)
        o_ref[pl.ds(t, 1), :] = x_ref[pl.ds(t, 1), :] + flat

    _pipelined_tokens(tb, functools.partial(_gather_rows, idx_ref, tab_ref), finish, slab_a, slab_b,
                      unroll=2 * GATHER_UNROLL)


PEER_TB = PEER_N_HEADS * GATHER_UNROLL


def _peer_layer(x, g, w_query, sub_keys, expert_u, expert_v):
    t, d = x.shape
    tb = PEER_TB
    nb = t // tb
    na = nb // 2
    ta = na * tb
    wq_t = w_query.T.astype(BF16)
    nq = wq_t.shape[0]
    keys = sub_keys.reshape(2 * PEER_N_HEADS, PEER_N_KEYS, PEER_D_HALF).astype(BF16)
    u_packed, v_packed = _pack_table(expert_u), _pack_table(expert_v)
    diag, spread = _slab_maps()
    spread_t = spread.T

    tok = pl.BlockSpec((tb, d), lambda i: (i, 0))
    tok3 = pl.BlockSpec((tb, SUBLANES, LANES), lambda i: (i, 0, 0))
    tok2 = pl.BlockSpec((tb, PEER_SLOTS), lambda i: (i, 0))
    rep2 = pl.BlockSpec((tb, SLAB_COLS), lambda i: (i, 0))
    smem1 = pl.BlockSpec((tb * PEER_SLOTS,), lambda i: (i,), memory_space=pltpu.SMEM)
    table = pl.BlockSpec(u_packed.shape, lambda i: (0, 0), pipeline_mode=pl.Buffered(1))
    const = lambda a: pl.BlockSpec(a.shape, lambda i: (0,) * a.ndim)
    slab = pltpu.VMEM((SLAB_ROWS, LANES), jnp.uint32)
    rbuf = pltpu.VMEM((tb, SLAB_COLS), F32)
    g2 = g.reshape(1, d)

    h_a, rows_a, gate_a = _peer_route(x, g, wq_t, keys, ta, tn=tb)
    crep_a, h_b, rows_b, gate_b = pl.pallas_call(
        _peer_u_route_kernel,
        grid=(na,),
        in_specs=[smem1, tok3, tok2, table, const(diag), const(spread), const(spread_t),
                  pl.BlockSpec((tb, d), lambda i: (i + na, 0)), const(g2), const(wq_t), const(keys)],
        out_specs=[rep2, tok, tok2, tok2],
        out_shape=[jax.ShapeDtypeStruct((ta, SLAB_COLS), F32), jax.ShapeDtypeStruct((t - ta, d), F32),
                   jax.ShapeDtypeStruct((t - ta, PEER_SLOTS), jnp.int32),
                   jax.ShapeDtypeStruct((t - ta, PEER_SLOTS), F32)],
        scratch_shapes=[slab, slab, rbuf, pltpu.VMEM((nq, tb), BF16),
                        pltpu.VMEM((PEER_SLOTS, tb), F32), pltpu.VMEM((PEER_SLOTS, tb), F32)],
        compiler_params=_params("arbitrary"),
        name="peer_u_route",
    )(rows_a.reshape(-1), h_a.reshape(ta, SUBLANES, LANES), gate_a, u_packed, diag, spread, spread_t,
      x, g2, wq_t, keys)
    crep_b = pl.pallas_call(
        _peer_u_kernel,
        grid=(nb - na,),
        in_specs=[smem1, tok3, tok2, table, const(diag), const(spread), const(spread_t)],
        out_specs=rep2,
        out_shape=jax.ShapeDtypeStruct((t - ta, SLAB_COLS), F32),
        scratch_shapes=[slab, slab, rbuf],
        compiler_params=_params("arbitrary"),
        name="peer_u",
    )(rows_b.reshape(-1), h_b.reshape(t - ta, SUBLANES, LANES), gate_b, u_packed, diag, spread, spread_t)
    rows = jnp.concatenate([rows_a, rows_b], axis=0).reshape(-1)
    return pl.pallas_call(
        functools.partial(_peer_v_kernel, blocks_a=na),
        grid=(nb,),
        in_specs=[smem1, pl.BlockSpec((tb, SLAB_COLS), lambda i: (jnp.minimum(i, na - 1), 0)),
                  pl.BlockSpec((tb, SLAB_COLS), lambda i: (jnp.maximum(i - na, 0), 0)), tok, table, const(diag)],
        out_specs=tok,
        out_shape=jax.ShapeDtypeStruct((t, d), F32),
        scratch_shapes=[slab, slab],
        compiler_params=_params("arbitrary"),
        name="peer_v",
    )(rows, crep_a, crep_b, x, v_packed, diag)


def _hgrn_lower_bound(lb_logits, layer):
    p = jax.nn.softmax(lb_logits.astype(F32), axis=0)
    return jnp.cumsum(p, axis=0)[layer] - p[0]


def kernel(x, mix_norm_g, ffn_norm_g, final_norm_g, ssd_w_in, ssd_conv_w, ssd_conv_b, ssd_dt_bias, ssd_a_log, ssd_d_skip, ssd_norm_g, ssd_w_out, hgrn_w_in, hgrn_lb_logits, hgrn_norm_g, hgrn_w_out, peer_w_query, peer_sub_keys, peer_u, peer_v):
    bsz, s, d = x.shape
    depth = mix_norm_g.shape[0]
    xf = x.reshape(bsz * s, d)
    for layer in range(depth):
        j = layer // 2
        if layer % 2 == 0:
            xf = _ssd_layer(xf, bsz, mix_norm_g[layer], ssd_w_in[j], ssd_conv_w[j], ssd_conv_b[j],
                            ssd_dt_bias[j], ssd_a_log[j], ssd_d_skip[j], ssd_norm_g[j], ssd_w_out[j])
        else:
            lb = _hgrn_lower_bound(hgrn_lb_logits, layer)
            xf = _hgrn_layer(xf, bsz, mix_norm_g[layer], hgrn_w_in[j], lb, hgrn_norm_g[j], hgrn_w_out[j])
        xf = _peer_layer(xf, ffn_norm_g[layer], peer_w_query[layer], peer_sub_keys[layer],
                         peer_u[layer], peer_v[layer])
    return _final_norm(xf, final_norm_g).reshape(bsz, s, d)
```

```python
import functools

import jax
import jax.numpy as jnp
from jax import lax
from jax.experimental import pallas as pl
from jax.experimental.pallas import tpu as pltpu

F32 = jnp.float32
BF16 = jnp.bfloat16
HIGHEST = lax.Precision.HIGHEST

LANES = 128
SUBLANES = 8
VMEM_LIMIT = 56 * 1024 * 1024

D_MODEL = 1024
NORM_EPS = 1e-6

SSD_D_INNER = 2048
SSD_HEAD_DIM = 64
SSD_N_HEADS = 32
SSD_N_GROUPS = 4
SSD_HEADS_PER_GROUP = 8
SSD_D_STATE = 128
SSD_CONV_WIDTH = 4
SSD_CONV_DIM = SSD_D_INNER + 2 * SSD_N_GROUPS * SSD_D_STATE
SSD_GROUP_DIM = SSD_D_INNER // SSD_N_GROUPS
CHUNK = 128

HGRN_N_HEADS = 8
HGRN_HEAD_DIM = 128

PEER_N_KEYS = 128
PEER_N_HEADS = 8
PEER_D_HALF = 128
PEER_TOPK = 16
PEER_SLOTS = PEER_N_HEADS * PEER_TOPK
ROW_WORDS = D_MODEL // 2
ROW_SUB = ROW_WORDS // LANES


def _rmsnorm(x, g):
    r = lax.rsqrt(jnp.mean(x * x, axis=-1, keepdims=True) + NORM_EPS)
    return (x * r) * g


def _silu(x):
    return x * (1.0 / (1.0 + jnp.exp(-x)))


def _sigmoid(x):
    return 1.0 / (1.0 + jnp.exp(-x))


def _params(*sem):
    return pltpu.CompilerParams(dimension_semantics=sem, vmem_limit_bytes=VMEM_LIMIT)


def _norm_matmul_kernel(x_ref, g_ref, w_ref, o_ref, h_scr):
    @pl.when(pl.program_id(1) == 0)
    def _():
        h_scr[...] = _rmsnorm(x_ref[...], g_ref[...]).astype(BF16)

    o_ref[...] = jnp.dot(h_scr[...], w_ref[...], preferred_element_type=F32).astype(o_ref.dtype)


def _norm_matmul(x, g, w, tm=1024, tn=1024, out_dtype=BF16):
    t, d = x.shape
    n = w.shape[1]
    return pl.pallas_call(
        _norm_matmul_kernel,
        grid=(t // tm, n // tn),
        in_specs=[pl.BlockSpec((tm, d), lambda i, j: (i, 0)),
                  pl.BlockSpec((1, d), lambda i, j: (0, 0)),
                  pl.BlockSpec((d, tn), lambda i, j: (0, j))],
        out_specs=pl.BlockSpec((tm, tn), lambda i, j: (i, j)),
        out_shape=jax.ShapeDtypeStruct((t, n), out_dtype),
        scratch_shapes=[pltpu.VMEM((tm, d), BF16)],
        compiler_params=_params("parallel", "arbitrary"),
        name="norm_matmul",
    )(x, g.reshape(1, d), w)


def _matmul_residual_kernel(x_ref, a_ref, w_ref, o_ref):
    o_ref[...] = x_ref[...] + jnp.dot(a_ref[...], w_ref[...], preferred_element_type=F32)


def _matmul_residual(x, a, w, tm=512):
    t, d = x.shape
    k = a.shape[1]
    return pl.pallas_call(
        _matmul_residual_kernel,
        grid=(t // tm,),
        in_specs=[pl.BlockSpec((tm, d), lambda i: (i, 0)),
                  pl.BlockSpec((tm, k), lambda i: (i, 0)),
                  pl.BlockSpec((k, d), lambda i: (0, 0))],
        out_specs=pl.BlockSpec((tm, d), lambda i: (i, 0)),
        out_shape=jax.ShapeDtypeStruct((t, d), F32),
        compiler_params=_params("parallel"),
        name="matmul_residual",
    )(x, a, w)


def _final_norm_kernel(x_ref, g_ref, o_ref):
    o_ref[...] = _rmsnorm(x_ref[...], g_ref[...])


def _final_norm(x, g, tm=1024):
    t, d = x.shape
    return pl.pallas_call(
        _final_norm_kernel,
        grid=(t // tm,),
        in_specs=[pl.BlockSpec((tm, d), lambda i: (i, 0)),
                  pl.BlockSpec((1, d), lambda i: (0, 0))],
        out_specs=pl.BlockSpec((tm, d), lambda i: (i, 0)),
        out_shape=jax.ShapeDtypeStruct((t, d), F32),
        compiler_params=_params("parallel"),
        name="final_norm",
    )(x, g.reshape(1, d))


def _ssd_kernel(x_ref, zx_ref, gmix_ref, wdt_ref, dtb_ref, alog_ref, convw_ref, convb_ref,
                dskip_ref, expand_ref, gnorm_ref, wout_ref, o_ref, state_scr, tail_scr):
    L = CHUNK
    G, N, P = SSD_N_GROUPS, SSD_D_STATE, SSD_HEAD_DIM
    GD = SSD_GROUP_DIM

    @pl.when(pl.program_id(1) == 0)
    def _():
        state_scr[...] = jnp.zeros_like(state_scr)
        tail_scr[...] = jnp.zeros_like(tail_scr)

    x = x_ref[...]
    h = _rmsnorm(x, gmix_ref[...]).astype(BF16)
    dt_raw = jnp.dot(h, wdt_ref[...], preferred_element_type=F32) + dtb_ref[...]
    dt = jnp.maximum(dt_raw, 0.0) + jnp.log(1.0 + jnp.exp(-jnp.abs(dt_raw)))
    a = -jnp.exp(alog_ref[...])
    row = lax.broadcasted_iota(jnp.int32, (L, L), 0)
    col = lax.broadcasted_iota(jnp.int32, (L, L), 1)
    causal = row >= col
    tril = causal.astype(F32)
    acs = jnp.dot(tril, dt * a, preferred_element_type=F32, precision=HIGHEST)
    acs_t = acs.T
    dt_t = dt.T
    exp_acs = jnp.exp(acs)
    tail = jnp.exp(acs[L - 1:L, :] - acs) * dt
    both = jnp.concatenate([exp_acs, tail], axis=0)
    both_e = jnp.dot(both, expand_ref[...], preferred_element_type=F32, precision=HIGHEST)
    exp_acs_e = both_e[:L]
    tail_e = both_e[L:]

    zx = zx_ref[...]
    z = zx[:, :SSD_D_INNER].astype(F32)
    xbc = zx[:, SSD_D_INNER:].astype(F32)
    ext = jnp.concatenate([tail_scr[...], xbc], axis=0)
    tail_scr[...] = xbc[L - SUBLANES:, :]
    conv = convb_ref[...] + convw_ref[SSD_CONV_WIDTH - 1:SSD_CONV_WIDTH, :] * xbc
    for k in range(SSD_CONV_WIDTH - 1):
        off = SUBLANES - (SSD_CONV_WIDTH - 1) + k
        conv = conv + convw_ref[k:k + 1, :] * ext[off:off + L, :]
    xbc = _silu(conv)
    xs = xbc[:, :SSD_D_INNER]
    bm = xbc[:, SSD_D_INNER:SSD_D_INNER + G * N]
    cm = xbc[:, SSD_D_INNER + G * N:]

    xs_tail = (xs * tail_e).astype(BF16)
    xs_b = xs.astype(BF16)
    y_parts = []
    for g in range(G):
        b_g = bm[:, g * N:(g + 1) * N].astype(BF16)
        c_g = cm[:, g * N:(g + 1) * N].astype(BF16)
        cb = lax.dot_general(c_g, b_g, (((1,), (1,)), ((), ())), preferred_element_type=F32)
        st = state_scr[g]
        y_inter = jnp.dot(c_g, st.astype(BF16), preferred_element_type=F32)
        y_g = y_inter * exp_acs_e[:, g * GD:(g + 1) * GD]
        intra = []
        for j in range(SSD_HEADS_PER_GROUP):
            hh = g * SSD_HEADS_PER_GROUP + j
            seg = acs[:, hh:hh + 1] - acs_t[hh:hh + 1, :]
            decay = jnp.exp(jnp.where(causal, seg, -jnp.inf))
            w = (cb * decay * dt_t[hh:hh + 1, :]).astype(BF16)
            intra.append(jnp.dot(w, xs_b[:, hh * P:(hh + 1) * P], preferred_element_type=F32))
        y_g = y_g + jnp.concatenate(intra, axis=1)
        y_parts.append(y_g)
        upd = lax.dot_general(b_g, xs_tail[:, g * GD:(g + 1) * GD], (((0,), (0,)), ((), ())),
                              preferred_element_type=F32)
        state_scr[g] = st * exp_acs_e[L - 1:L, g * GD:(g + 1) * GD] + upd
    y = jnp.concatenate(y_parts, axis=1)
    y = y + xs * dskip_ref[...]
    y = y * _silu(z)
    normed = []
    for g in range(G):
        y_g = y[:, g * GD:(g + 1) * GD]
        normed.append(_rmsnorm(y_g, gnorm_ref[:, g * GD:(g + 1) * GD]))
    yn = jnp.concatenate(normed, axis=1).astype(BF16)
    o_ref[...] = x + jnp.dot(yn, wout_ref[...], preferred_element_type=F32)


def _pad_lanes(v):
    return jnp.pad(v.astype(F32), (0, LANES - v.shape[0])).reshape(1, LANES)


def _ssd_layer(x, bsz, g_mix, w_in, conv_w, conv_b, dt_bias, a_log, d_skip, norm_g, w_out):
    t, d = x.shape
    nc = t // bsz // CHUNK
    w_zx = w_in[:, :SSD_D_INNER + SSD_CONV_DIM].astype(BF16)
    w_dt = jnp.pad(w_in[:, SSD_D_INNER + SSD_CONV_DIM:], ((0, 0), (0, LANES - SSD_N_HEADS))).astype(BF16)
    zx = _norm_matmul(x, g_mix, w_zx)
    head_of_lane = jnp.arange(SSD_D_INNER) // SSD_HEAD_DIM
    expand = (jnp.arange(LANES)[:, None] == head_of_lane[None, :]).astype(F32)
    dskip_e = jnp.repeat(d_skip.astype(F32), SSD_HEAD_DIM).reshape(1, SSD_D_INNER)
    const = lambda shape: pl.BlockSpec(shape, lambda b, c: (0,) * len(shape))
    tok = lambda width: pl.BlockSpec((CHUNK, width), lambda b, c: (b * nc + c, 0))
    return pl.pallas_call(
        _ssd_kernel,
        grid=(bsz, nc),
        in_specs=[tok(d), tok(SSD_D_INNER + SSD_CONV_DIM), const((1, d)), const((d, LANES)),
                  const((1, LANES)), const((1, LANES)), const((SSD_CONV_WIDTH, SSD_CONV_DIM)),
                  const((1, SSD_CONV_DIM)), const((1, SSD_D_INNER)), const((LANES, SSD_D_INNER)),
                  const((1, SSD_D_INNER)), const((SSD_D_INNER, d))],
        out_specs=tok(d),
        out_shape=jax.ShapeDtypeStruct((t, d), F32),
        scratch_shapes=[pltpu.VMEM((SSD_N_GROUPS, SSD_D_STATE, SSD_GROUP_DIM), F32),
                        pltpu.VMEM((SUBLANES, SSD_CONV_DIM), F32)],
        compiler_params=_params("parallel", "arbitrary"),
        name="ssd_scan",
    )(x, zx, g_mix.reshape(1, d), w_dt, _pad_lanes(dt_bias), _pad_lanes(a_log), conv_w.astype(F32),
      conv_b.reshape(1, -1).astype(F32), dskip_e, expand, norm_g.reshape(1, -1).astype(F32),
      w_out.astype(BF16))


HGRN_BLOCK = SUBLANES
HGRN_PAIRS = CHUNK * HGRN_BLOCK
HGRN_HEADS_PER_STEP = 8


def _split3(x):
    x1 = x.astype(BF16)
    r1 = x - x1.astype(F32)
    x2 = r1.astype(BF16)
    x3 = (r1 - x2.astype(F32)).astype(BF16)
    return x1, x2, x3


def _hgrn_kernel(q_ref, f_ref, i_ref, g_ref, lb_ref, gnorm_ref, rep_ref, tile_ref, dif_ref, fold_ref,
                 o_ref, state_scr):
    L, C = CHUNK, HGRN_BLOCK
    NB = L // C
    K = HGRN_HEAD_DIM
    H = state_scr.shape[0]
    W = H * K
    nt = (((1,), (1,)), ((), ()))

    @pl.when(pl.program_id(2) == 0)
    def _():
        state_scr[...] = jnp.zeros_like(state_scr)

    lb = lb_ref[...]
    fl = f_ref[...].astype(F32)
    logf = jnp.log(lb + (1.0 - lb) * _sigmoid(fl))
    kk = (1.0 - lb) * _sigmoid(-fl)
    q = q_ref[...].astype(F32)
    row = lax.broadcasted_iota(jnp.int32, (L, L), 0)
    col = lax.broadcasted_iota(jnp.int32, (L, L), 1)
    b = jnp.dot((row >= col).astype(F32), logf, preferred_element_type=F32, precision=HIGHEST)
    b3 = b.reshape(NB, C, W)
    b_end3 = jnp.broadcast_to(b3[:, C - 1:C, :], b3.shape)
    b_prev3 = jnp.concatenate([jnp.zeros((1, C, W), F32), b_end3[:NB - 1]], axis=0)
    b_end = b_end3.reshape(L, W)
    b_prev = b_prev3.reshape(L, W)
    q_dec = (q * jnp.exp(b - b_prev)).astype(BF16)
    k_dec = (kk * jnp.exp(b_end - b)).astype(BF16)
    blk_dec = jnp.exp(b_end - b_prev)

    v_t = i_ref[...].astype(F32).T
    lane_blk = lax.broadcasted_iota(jnp.int32, (K, L), 1) // C
    inter = []
    for h in range(H):
        cols = slice(h * K, (h + 1) * K)
        v_h, k_h, q_h = v_t[cols], k_dec[:, cols], q_dec[:, cols]
        outer = [jnp.dot(jnp.where(lane_blk == i, v_h, 0.0).astype(BF16), k_h, preferred_element_type=F32)
                 for i in range(NB)]
        s = state_scr[h]
        o_h = jnp.zeros((K, L), F32)
        for i in range(NB):
            r = lax.dot_general(s.astype(BF16), q_h, nt, preferred_element_type=F32)
            o_h = jnp.where(lane_blk == i, r, o_h)
            s = s * blk_dec[i * C:i * C + 1, cols] + outer[i]
        state_scr[h] = s
        inter.append(o_h)

    q_e = jnp.dot(q.T.astype(BF16), rep_ref[...], preferred_element_type=F32)
    k_e = jnp.dot(kk.T.astype(BF16), tile_ref[...], preferred_element_type=F32)
    dif = dif_ref[...]
    seg = sum(jnp.dot(part, dif, preferred_element_type=F32) for part in _split3(b.T))
    j = lax.broadcasted_iota(jnp.int32, (W, HGRN_PAIRS), 1)
    ordered = (j % C) <= ((j // C) % C)
    prod = q_e * k_e * jnp.exp(jnp.where(ordered, seg, -jnp.inf))
    v_e = jnp.dot(v_t.astype(BF16), tile_ref[...], preferred_element_type=F32)
    weighted = [v_e[h * K:(h + 1) * K] * jnp.sum(prod[h * K:(h + 1) * K], axis=0, keepdims=True)
                for h in range(H)]
    o_t = jnp.concatenate(inter, axis=0) + jnp.dot(jnp.concatenate(weighted, axis=0).astype(BF16),
                                                   fold_ref[...], preferred_element_type=F32)
    o = o_t.T
    gate = _silu(g_ref[...].astype(F32))
    gnorm = gnorm_ref[...]
    for h in range(H):
        cols = slice(h * K, (h + 1) * K)
        o_ref[:, cols] = (_rmsnorm(o[:, cols], gnorm) * gate[:, cols]).astype(o_ref.dtype)


def _hgrn_pair_maps():
    c = HGRN_BLOCK
    j = jnp.arange(HGRN_PAIRS)
    t_of = j // c
    s_of = (t_of // c) * c + j % c
    t = jnp.arange(CHUNK)[:, None]
    rep = (t == t_of[None, :]).astype(F32)
    tile = (t == s_of[None, :]).astype(F32)
    return rep.astype(BF16), tile.astype(BF16), (rep - tile).astype(BF16), rep.T.astype(BF16)


def _hgrn_layer(x, bsz, g_mix, w_in, lower_bound, norm_g, w_out):
    t, d = x.shape
    nc = t // bsz // CHUNK
    hd = HGRN_HEAD_DIM
    qfig = _norm_matmul(x, g_mix, w_in.astype(BF16))
    hs = HGRN_HEADS_PER_STEP
    groups = HGRN_N_HEADS // hs
    part = lambda k: pl.BlockSpec((CHUNK, hs * hd), lambda b, h, c: (b * nc + c, k * groups + h))
    const = lambda shape: pl.BlockSpec(shape, lambda b, h, c: (0, 0))
    rep, tile, dif, fold = _hgrn_pair_maps()
    o = pl.pallas_call(
        _hgrn_kernel,
        grid=(bsz, groups, nc),
        in_specs=[part(0), part(1), part(2), part(3),
                  pl.BlockSpec((1, hs * hd), lambda b, h, c: (0, h)), const((1, hd)),
                  const(rep.shape), const(tile.shape), const(dif.shape), const(fold.shape)],
        out_specs=pl.BlockSpec((CHUNK, hs * hd), lambda b, h, c: (b * nc + c, h)),
        out_shape=jax.ShapeDtypeStruct((t, d), BF16),
        scratch_shapes=[pltpu.VMEM((hs, hd, hd), F32)],
        compiler_params=_params("parallel", "parallel", "arbitrary"),
        name="hgrn_scan",
    )(qfig, qfig, qfig, qfig, lower_bound.reshape(1, d).astype(F32), norm_g.reshape(1, hd).astype(F32),
      rep, tile, dif, fold)
    return _matmul_residual(x, o, w_out.astype(BF16))


def _drain(steps):
    try:
        while True:
            next(steps)
    except StopIteration as done:
        return done.value


def _topk_rows_steps(s, k, payload=None):
    r = s.shape[0]
    rows = lax.broadcasted_iota(jnp.int32, s.shape, 0).astype(F32)
    vals, picks = [], []
    for _ in range(k):
        m = jnp.max(s, axis=0, keepdims=True)
        at_max = jnp.where(s == m, rows, float(r))
        pos = jnp.min(at_max, axis=0, keepdims=True)
        hit = at_max == pos
        vals.append(m)
        picks.append(pos if payload is None else jnp.sum(jnp.where(hit, payload, 0.0), axis=0, keepdims=True))
        s = jnp.where(hit, -jnp.inf, s)
        yield
    return jnp.concatenate(vals, axis=0), jnp.concatenate(picks, axis=0)


def _topk_rows(s, k, payload=None):
    return _drain(_topk_rows_steps(s, k, payload))


def _route_head_steps(keys_ref, qt_ref, idx_scr, gate_scr, head):
    top_s, top_i = [], []
    for c in range(2):
        hc = head * 2 + c
        q_hc = qt_ref[pl.ds(pl.multiple_of(hc * PEER_D_HALF, PEER_D_HALF), PEER_D_HALF), :]
        sc = jnp.dot(keys_ref[hc], q_hc, preferred_element_type=F32)
        s, i = yield from _topk_rows_steps(sc, PEER_TOPK)
        top_s.append(s)
        top_i.append(i)
    cand_s = _product_candidates(top_s[0], top_s[1], lambda a, b: a + b)
    cand_i = _product_candidates(top_i[0], top_i[1], lambda a, b: a * float(PEER_N_KEYS) + b)
    best_s, best_i = yield from _topk_rows_steps(cand_s, PEER_TOPK, payload=cand_i)
    e = jnp.exp(best_s - best_s[0:1, :])
    out_rows = pl.ds(pl.multiple_of(head * PEER_TOPK, PEER_TOPK), PEER_TOPK)
    idx_scr[out_rows, :] = best_i
    gate_scr[out_rows, :] = e / jnp.sum(e, axis=0, keepdims=True)


def _product_candidates(first, second, combine):
    half = PEER_TOPK // 2
    parts = [combine(first[0:1, :], second[0:half, :]), combine(first[0:1, :], second[half:, :])]
    parts += [combine(first[a:a + 1, :], second[0:half, :]) for a in range(1, half)]
    parts.append(combine(first[half:, :], second[0:1, :]))
    return jnp.concatenate(parts, axis=0)


def _peer_route_kernel(x_ref, g_ref, wq_ref, keys_ref, h_ref, idx_ref, gate_ref):
    h = _rmsnorm(x_ref[...], g_ref[...])
    h_ref[...] = h
    q_t = lax.dot_general(wq_ref[...], h.astype(BF16), (((1,), (1,)), ((), ())),
                          preferred_element_type=F32)
    idx_rows, gate_rows = [], []
    for head in range(PEER_N_HEADS):
        top_s, top_i = [], []
        for c in range(2):
            hc = head * 2 + c
            q_hc = q_t[hc * PEER_D_HALF:(hc + 1) * PEER_D_HALF, :].astype(BF16)
            sc = jnp.dot(keys_ref[hc], q_hc, preferred_element_type=F32)
            s, i = _topk_rows(sc, PEER_TOPK)
            top_s.append(s)
            top_i.append(i)
        cand_s = _product_candidates(top_s[0], top_s[1], lambda a, b: a + b)
        cand_i = _product_candidates(top_i[0], top_i[1], lambda a, b: a * float(PEER_N_KEYS) + b)
        best_s, best_i = _topk_rows(cand_s, PEER_TOPK, payload=cand_i)
        e = jnp.exp(best_s - best_s[0:1, :])
        gate_rows.append(e / jnp.sum(e, axis=0, keepdims=True))
        idx_rows.append(best_i)
    idx_ref[...] = jnp.concatenate(idx_rows, axis=0).T.astype(jnp.int32) * ROW_SUB
    gate_ref[...] = jnp.concatenate(gate_rows, axis=0).T


def _peer_route(x, g, wq_t, keys, n_tokens, tn=128):
    t, d = x.shape
    t = n_tokens
    nq = wq_t.shape[0]
    return pl.pallas_call(
        _peer_route_kernel,
        grid=(t // tn,),
        in_specs=[pl.BlockSpec((tn, d), lambda i: (i, 0)),
                  pl.BlockSpec((1, d), lambda i: (0, 0)),
                  pl.BlockSpec((nq, d), lambda i: (0, 0)),
                  pl.BlockSpec(keys.shape, lambda i: (0, 0, 0))],
        out_specs=[pl.BlockSpec((tn, d), lambda i: (i, 0)),
                   pl.BlockSpec((tn, PEER_SLOTS), lambda i: (i, 0)),
                   pl.BlockSpec((tn, PEER_SLOTS), lambda i: (i, 0))],
        out_shape=[jax.ShapeDtypeStruct((t, d), F32),
                   jax.ShapeDtypeStruct((t, PEER_SLOTS), jnp.int32),
                   jax.ShapeDtypeStruct((t, PEER_SLOTS), F32)],
        compiler_params=_params("parallel"),
        name="peer_route",
    )(x, g.reshape(1, d), wq_t, keys)


SLAB_ROWS = PEER_SLOTS * ROW_SUB
SLAB_COLS = 2 * SLAB_ROWS
GATHER_UNROLL = 16


def _pack_table(t):
    b = lax.bitcast_convert_type(t.astype(BF16), jnp.uint16).astype(jnp.uint32)
    w = b[:, :ROW_WORDS] | (b[:, ROW_WORDS:] << 16)
    return w.reshape(-1, LANES)


def _slab_maps():
    col = jnp.arange(SLAB_COLS)
    row = jnp.arange(SUBLANES)[:, None]
    diag = (row == ((col // 2) % ROW_SUB + ROW_SUB * (col % 2))[None, :]).astype(F32)
    spread = (jnp.arange(PEER_SLOTS)[:, None] == (col // SUBLANES)[None, :]).astype(BF16)
    return diag, spread


SIDE_EVERY = 32


def _gather_rows(idx_ref, tab_ref, slab, t, side=None):
    base = t * PEER_SLOTS
    for p in range(PEER_SLOTS):
        row = pl.multiple_of(idx_ref[base + p], ROW_SUB)
        slab[p * ROW_SUB:(p + 1) * ROW_SUB, :] = tab_ref[pl.ds(row, ROW_SUB), :]
        if side is not None and p % SIDE_EVERY == SIDE_EVERY - 1:
            next(side, None)


def _pipelined_tokens(tb, gather, finish, slab_a, slab_b, unroll=GATHER_UNROLL, side_work=None):
    gather(slab_a, 0)
    slabs = (slab_a, slab_b)

    def group(i, carry):
        t0 = unroll * i
        side = None if side_work is None else side_work(i)
        for k in range(unroll):
            gather(slabs[(k + 1) % 2], jnp.minimum(t0 + k + 1, tb - 1), side)
            finish(slabs[k % 2], t0 + k)
        if side is not None:
            _drain(side)
        return carry

    lax.fori_loop(0, tb // unroll, group, 0)


_NT = (((1,), (1,)), ((), ()))


def _peer_u_kernel(idx_ref, h_ref, gate_ref, tab_ref, diag_ref, spread_ref, spread_t_ref, crep_all_ref,
                   crep_ref, slab_a, slab_b, rbuf, side_work=None):
    tb = h_ref.shape[0]
    diag = diag_ref[...]

    def finish(slab, t):
        w = pltpu.bitcast(slab[...], BF16)
        r = lax.dot_general(h_ref[t].astype(BF16), w, _NT, preferred_element_type=F32)
        rbuf[pl.ds(t, 1), :] = jnp.sum(r * diag, axis=0, keepdims=True)

    _pipelined_tokens(tb, functools.partial(_gather_rows, idx_ref, tab_ref), finish, slab_a, slab_b,
                      side_work=side_work)
    spread_t = spread_t_ref[...]
    act = sum(jnp.dot(part, spread_t, preferred_element_type=F32) for part in _split3(rbuf[...]))
    gelu = 0.5 * act * (1.0 + lax.erf(act * (2.0 ** -0.5)))
    c = (gate_ref[...] * gelu).astype(BF16)
    crep_ref[...] = jnp.dot(c, spread_ref[...], preferred_element_type=F32)


def _peer_u_route_kernel(idx_ref, h_ref, gate_ref, tab_ref, diag_ref, spread_ref, spread_t_ref, crep_all_ref,
                         x2_ref, g_ref, wq_ref, keys_ref, crep_ref, h2_ref, idx2_ref, gate2_ref,
                         slab_a, slab_b, rbuf, qt_scr, idx_scr, gate_scr):
    h2 = _rmsnorm(x2_ref[...], g_ref[...])
    h2_ref[...] = h2
    qt_scr[...] = lax.dot_general(wq_ref[...], h2.astype(BF16), _NT, preferred_element_type=F32).astype(BF16)
    _peer_u_kernel(idx_ref, h_ref, gate_ref, tab_ref, diag_ref, spread_ref, spread_t_ref, crep_all_ref,
                   crep_ref, slab_a, slab_b, rbuf,
                   side_work=functools.partial(_route_head_steps, keys_ref, qt_scr, idx_scr, gate_scr))
    idx2_ref[...] = idx_scr[...].T.astype(jnp.int32) * ROW_SUB
    gate2_ref[...] = gate_scr[...].T


def _peer_v_kernel(idx_ref, crep_ref, x_ref, tab_ref, diag_ref, o_ref, slab_a, slab_b):
    tb = x_ref.shape[0]
    diag = diag_ref[...]

    def finish(slab, t):
        w = pltpu.bitcast(slab[...], BF16)
        c = (crep_ref[pl.ds(t, 1), :] * diag).astype(BF16)
        y = jnp.dot(c, w, preferred_element_type=F32)
        flat = jnp.concatenate([y[s:s + 1, :] for s in range(SUBLANES)], axis=1)
        o_ref[pl.ds(t, 1), :] = x_ref[pl.ds(t, 1), :] + flat

    _pipelined_tokens(tb, functools.partial(_gather_rows, idx_ref, tab_ref), finish, slab_a, slab_b)


PEER_TB = PEER_N_HEADS * GATHER_UNROLL
PEER_PARTS = 4


def _peer_layer(x, g, w_query, sub_keys, expert_u, expert_v):
    t, d = x.shape
    tb = PEER_TB
    nb = t // tb
    parts = PEER_PARTS if nb % PEER_PARTS == 0 else 2
    npb = nb // parts
    tp = npb * tb
    wq_t = w_query.T.astype(BF16)
    nq = wq_t.shape[0]
    keys = sub_keys.reshape(2 * PEER_N_HEADS, PEER_N_KEYS, PEER_D_HALF).astype(BF16)
    u_packed, v_packed = _pack_table(expert_u), _pack_table(expert_v)
    diag, spread = _slab_maps()
    spread_t = spread.T

    tok = pl.BlockSpec((tb, d), lambda i: (i, 0))
    tok3 = pl.BlockSpec((tb, SUBLANES, LANES), lambda i: (i, 0, 0))
    tok2 = pl.BlockSpec((tb, PEER_SLOTS), lambda i: (i, 0))
    rep2 = pl.BlockSpec((tb, SLAB_COLS), lambda i: (i, 0))
    smem1 = pl.BlockSpec((tb * PEER_SLOTS,), lambda i: (i,), memory_space=pltpu.SMEM)
    table = pl.BlockSpec(u_packed.shape, lambda i: (0, 0), pipeline_mode=pl.Buffered(1))
    const = lambda a: pl.BlockSpec(a.shape, lambda i: (0,) * a.ndim)
    slab = pltpu.VMEM((SLAB_ROWS, LANES), jnp.uint32)
    rbuf = pltpu.VMEM((tb, SLAB_COLS), F32)
    g2 = g.reshape(1, d)

    whole = pl.BlockSpec(memory_space=pl.ANY)
    crep_shape = jax.ShapeDtypeStruct((t, SLAB_COLS), F32)
    part_shapes = [jax.ShapeDtypeStruct((tp, d), F32), jax.ShapeDtypeStruct((tp, PEER_SLOTS), jnp.int32),
                   jax.ShapeDtypeStruct((tp, PEER_SLOTS), F32)]
    u_specs = [smem1, tok3, tok2, table, const(diag), const(spread), const(spread_t), whole]
    crep = jnp.zeros((t, SLAB_COLS), F32)
    h, rows, gate = _peer_route(x, g, wq_t, keys, tp, tn=tb)
    all_rows = [rows]
    for j in range(parts):
        crep_spec = pl.BlockSpec((tb, SLAB_COLS), lambda i, j=j: (i + j * npb, 0))
        u_args = (rows.reshape(-1), h.reshape(tp, SUBLANES, LANES), gate, u_packed, diag, spread, spread_t, crep)
        if j + 1 < parts:
            crep, h, rows, gate = pl.pallas_call(
                _peer_u_route_kernel,
                grid=(npb,),
                in_specs=u_specs + [pl.BlockSpec((tb, d), lambda i, j=j: (i + (j + 1) * npb, 0)),
                                    const(g2), const(wq_t), const(keys)],
                out_specs=[crep_spec, tok, tok2, tok2],
                out_shape=[crep_shape] + part_shapes,
                scratch_shapes=[slab, slab, rbuf, pltpu.VMEM((nq, tb), BF16),
                                pltpu.VMEM((PEER_SLOTS, tb), F32), pltpu.VMEM((PEER_SLOTS, tb), F32)],
                input_output_aliases={len(u_specs) - 1: 0},
                compiler_params=_params("arbitrary"),
                name="peer_u_route",
            )(*u_args, x, g2, wq_t, keys)
            all_rows.append(rows)
        else:
            crep = pl.pallas_call(
                _peer_u_kernel,
                grid=(npb,),
                in_specs=u_specs,
                out_specs=crep_spec,
                out_shape=crep_shape,
                scratch_shapes=[slab, slab, rbuf],
                input_output_aliases={len(u_specs) - 1: 0},
                compiler_params=_params("arbitrary"),
                name="peer_u",
            )(*u_args)
    return pl.pallas_call(
        _peer_v_kernel,
        grid=(nb,),
        in_specs=[smem1, rep2, tok, table, const(diag)],
        out_specs=tok,
        out_shape=jax.ShapeDtypeStruct((t, d), F32),
        scratch_shapes=[slab, slab],
        compiler_params=_params("arbitrary"),
        name="peer_v",
    )(jnp.concatenate(all_rows, axis=0).reshape(-1), crep, x, v_packed, diag)


def _hgrn_lower_bound(lb_logits, layer):
    p = jax.nn.softmax(lb_logits.astype(F32), axis=0)
    return jnp.cumsum(p, axis=0)[layer] - p[0]


def kernel(x, mix_norm_g, ffn_norm_g, final_norm_g, ssd_w_in, ssd_conv_w, ssd_conv_b, ssd_dt_bias, ssd_a_log, ssd_d_skip, ssd_norm_g, ssd_w_out, hgrn_w_in, hgrn_lb_logits, hgrn_norm_g, hgrn_w_out, peer_w_query, peer_sub_keys, peer_u, peer_v):
    bsz, s, d = x.shape
    depth = mix_norm_g.shape[0]
    xf = x.reshape(bsz * s, d)
    for layer in range(depth):
        j = layer // 2
        if layer % 2 == 0:
            xf = _ssd_layer(xf, bsz, mix_norm_g[layer], ssd_w_in[j], ssd_conv_w[j], ssd_conv_b[j],
                            ssd_dt_bias[j], ssd_a_log[j], ssd_d_skip[j], ssd_norm_g[j], ssd_w_out[j])
        else:
            lb = _hgrn_lower_bound(hgrn_lb_logits, layer)
            xf = _hgrn_layer(xf, bsz, mix_norm_g[layer], hgrn_w_in[j], lb, hgrn_norm_g[j], hgrn_w_out[j])
        xf = _peer_layer(xf, ffn_norm_g[layer], peer_w_query[layer], peer_sub_keys[layer],
                         peer_u[layer], peer_v[layer])
    return _final_norm(xf, final_norm_g).reshape(bsz, s, d)
```

```python
import functools

import jax
import jax.numpy as jnp
from jax import lax
from jax.experimental import pallas as pl
from jax.experimental.pallas import tpu as pltpu

F32 = jnp.float32
BF16 = jnp.bfloat16
HIGHEST = lax.Precision.HIGHEST

LANES = 128
SUBLANES = 8
VMEM_LIMIT = 56 * 1024 * 1024

D_MODEL = 1024
NORM_EPS = 1e-6

SSD_D_INNER = 2048
SSD_HEAD_DIM = 64
SSD_N_HEADS = 32
SSD_N_GROUPS = 4
SSD_HEADS_PER_GROUP = 8
SSD_D_STATE = 128
SSD_CONV_WIDTH = 4
SSD_CONV_DIM = SSD_D_INNER + 2 * SSD_N_GROUPS * SSD_D_STATE
SSD_GROUP_DIM = SSD_D_INNER // SSD_N_GROUPS
CHUNK = 128

HGRN_N_HEADS = 8
HGRN_HEAD_DIM = 128

PEER_N_KEYS = 128
PEER_N_HEADS = 8
PEER_D_HALF = 128
PEER_TOPK = 16
PEER_SLOTS = PEER_N_HEADS * PEER_TOPK
ROW_WORDS = D_MODEL // 2
ROW_SUB = ROW_WORDS // LANES


def _rmsnorm(x, g):
    r = lax.rsqrt(jnp.mean(x * x, axis=-1, keepdims=True) + NORM_EPS)
    return (x * r) * g


def _silu(x):
    return x * (1.0 / (1.0 + jnp.exp(-x)))


def _sigmoid(x):
    return 1.0 / (1.0 + jnp.exp(-x))


def _params(*sem):
    return pltpu.CompilerParams(dimension_semantics=sem, vmem_limit_bytes=VMEM_LIMIT)


def _norm_matmul_kernel(x_ref, g_ref, w_ref, o_ref, h_scr):
    @pl.when(pl.program_id(1) == 0)
    def _():
        h_scr[...] = _rmsnorm(x_ref[...], g_ref[...]).astype(BF16)

    o_ref[...] = jnp.dot(h_scr[...], w_ref[...], preferred_element_type=F32).astype(o_ref.dtype)


def _norm_matmul(x, g, w, tm=1024, tn=1024, out_dtype=BF16):
    t, d = x.shape
    n = w.shape[1]
    return pl.pallas_call(
        _norm_matmul_kernel,
        grid=(t // tm, n // tn),
        in_specs=[pl.BlockSpec((tm, d), lambda i, j: (i, 0)),
                  pl.BlockSpec((1, d), lambda i, j: (0, 0)),
                  pl.BlockSpec((d, tn), lambda i, j: (0, j))],
        out_specs=pl.BlockSpec((tm, tn), lambda i, j: (i, j)),
        out_shape=jax.ShapeDtypeStruct((t, n), out_dtype),
        scratch_shapes=[pltpu.VMEM((tm, d), BF16)],
        compiler_params=_params("parallel", "arbitrary"),
        name="norm_matmul",
    )(x, g.reshape(1, d), w)


def _matmul_residual_kernel(x_ref, a_ref, w_ref, o_ref):
    o_ref[...] = x_ref[...] + jnp.dot(a_ref[...], w_ref[...], preferred_element_type=F32)


def _matmul_residual(x, a, w, tm=512):
    t, d = x.shape
    k = a.shape[1]
    return pl.pallas_call(
        _matmul_residual_kernel,
        grid=(t // tm,),
        in_specs=[pl.BlockSpec((tm, d), lambda i: (i, 0)),
                  pl.BlockSpec((tm, k), lambda i: (i, 0)),
                  pl.BlockSpec((k, d), lambda i: (0, 0))],
        out_specs=pl.BlockSpec((tm, d), lambda i: (i, 0)),
        out_shape=jax.ShapeDtypeStruct((t, d), F32),
        compiler_params=_params("parallel"),
        name="matmul_residual",
    )(x, a, w)


def _final_norm_kernel(x_ref, g_ref, o_ref):
    o_ref[...] = _rmsnorm(x_ref[...], g_ref[...])


def _final_norm(x, g, tm=1024):
    t, d = x.shape
    return pl.pallas_call(
        _final_norm_kernel,
        grid=(t // tm,),
        in_specs=[pl.BlockSpec((tm, d), lambda i: (i, 0)),
                  pl.BlockSpec((1, d), lambda i: (0, 0))],
        out_specs=pl.BlockSpec((tm, d), lambda i: (i, 0)),
        out_shape=jax.ShapeDtypeStruct((t, d), F32),
        compiler_params=_params("parallel"),
        name="final_norm",
    )(x, g.reshape(1, d))


def _ssd_kernel(x_ref, zx_ref, gmix_ref, wdt_ref, dtb_ref, alog_ref, convw_ref, convb_ref,
                dskip_ref, expand_ref, gnorm_ref, wout_ref, o_ref, state_scr, tail_scr):
    L = CHUNK
    G, N, P = SSD_N_GROUPS, SSD_D_STATE, SSD_HEAD_DIM
    GD = SSD_GROUP_DIM

    @pl.when(pl.program_id(1) == 0)
    def _():
        state_scr[...] = jnp.zeros_like(state_scr)
        tail_scr[...] = jnp.zeros_like(tail_scr)

    x = x_ref[...]
    h = _rmsnorm(x, gmix_ref[...]).astype(BF16)
    dt_raw = jnp.dot(h, wdt_ref[...], preferred_element_type=F32) + dtb_ref[...]
    dt = jnp.maximum(dt_raw, 0.0) + jnp.log(1.0 + jnp.exp(-jnp.abs(dt_raw)))
    a = -jnp.exp(alog_ref[...])
    row = lax.broadcasted_iota(jnp.int32, (L, L), 0)
    col = lax.broadcasted_iota(jnp.int32, (L, L), 1)
    causal = row >= col
    tril = causal.astype(F32)
    acs = jnp.dot(tril, dt * a, preferred_element_type=F32, precision=HIGHEST)
    acs_t = acs.T
    dt_t = dt.T
    exp_acs = jnp.exp(acs)
    tail = jnp.exp(acs[L - 1:L, :] - acs) * dt
    both = jnp.concatenate([exp_acs, tail], axis=0)
    both_e = jnp.dot(both, expand_ref[...], preferred_element_type=F32, precision=HIGHEST)
    exp_acs_e = both_e[:L]
    tail_e = both_e[L:]

    zx = zx_ref[...]
    z = zx[:, :SSD_D_INNER].astype(F32)
    xbc = zx[:, SSD_D_INNER:].astype(F32)
    ext = jnp.concatenate([tail_scr[...], xbc], axis=0)
    tail_scr[...] = xbc[L - SUBLANES:, :]
    conv = convb_ref[...] + convw_ref[SSD_CONV_WIDTH - 1:SSD_CONV_WIDTH, :] * xbc
    for k in range(SSD_CONV_WIDTH - 1):
        off = SUBLANES - (SSD_CONV_WIDTH - 1) + k
        conv = conv + convw_ref[k:k + 1, :] * ext[off:off + L, :]
    xbc = _silu(conv)
    xs = xbc[:, :SSD_D_INNER]
    bm = xbc[:, SSD_D_INNER:SSD_D_INNER + G * N]
    cm = xbc[:, SSD_D_INNER + G * N:]

    xs_tail = (xs * tail_e).astype(BF16)
    xs_b = xs.astype(BF16)
    y_parts = []
    for g in range(G):
        b_g = bm[:, g * N:(g + 1) * N].astype(BF16)
        c_g = cm[:, g * N:(g + 1) * N].astype(BF16)
        cb = lax.dot_general(c_g, b_g, (((1,), (1,)), ((), ())), preferred_element_type=F32)
        st = state_scr[g]
        y_inter = jnp.dot(c_g, st.astype(BF16), preferred_element_type=F32)
        y_g = y_inter * exp_acs_e[:, g * GD:(g + 1) * GD]
        intra = []
        for j in range(SSD_HEADS_PER_GROUP):
            hh = g * SSD_HEADS_PER_GROUP + j
            seg = acs[:, hh:hh + 1] - acs_t[hh:hh + 1, :]
            decay = jnp.exp(jnp.where(causal, seg, -jnp.inf))
            w = (cb * decay * dt_t[hh:hh + 1, :]).astype(BF16)
            intra.append(jnp.dot(w, xs_b[:, hh * P:(hh + 1) * P], preferred_element_type=F32))
        y_g = y_g + jnp.concatenate(intra, axis=1)
        y_parts.append(y_g)
        upd = lax.dot_general(b_g, xs_tail[:, g * GD:(g + 1) * GD], (((0,), (0,)), ((), ())),
                              preferred_element_type=F32)
        state_scr[g] = st * exp_acs_e[L - 1:L, g * GD:(g + 1) * GD] + upd
    y = jnp.concatenate(y_parts, axis=1)
    y = y + xs * dskip_ref[...]
    y = y * _silu(z)
    normed = []
    for g in range(G):
        y_g = y[:, g * GD:(g + 1) * GD]
        normed.append(_rmsnorm(y_g, gnorm_ref[:, g * GD:(g + 1) * GD]))
    yn = jnp.concatenate(normed, axis=1).astype(BF16)
    o_ref[...] = x + jnp.dot(yn, wout_ref[...], preferred_element_type=F32)


def _pad_lanes(v):
    return jnp.pad(v.astype(F32), (0, LANES - v.shape[0])).reshape(1, LANES)


def _ssd_layer(x, bsz, g_mix, w_in, conv_w, conv_b, dt_bias, a_log, d_skip, norm_g, w_out):
    t, d = x.shape
    nc = t // bsz // CHUNK
    w_zx = w_in[:, :SSD_D_INNER + SSD_CONV_DIM].astype(BF16)
    w_dt = jnp.pad(w_in[:, SSD_D_INNER + SSD_CONV_DIM:], ((0, 0), (0, LANES - SSD_N_HEADS))).astype(BF16)
    zx = _norm_matmul(x, g_mix, w_zx)
    head_of_lane = jnp.arange(SSD_D_INNER) // SSD_HEAD_DIM
    expand = (jnp.arange(LANES)[:, None] == head_of_lane[None, :]).astype(F32)
    dskip_e = jnp.repeat(d_skip.astype(F32), SSD_HEAD_DIM).reshape(1, SSD_D_INNER)
    const = lambda shape: pl.BlockSpec(shape, lambda b, c: (0,) * len(shape))
    tok = lambda width: pl.BlockSpec((CHUNK, width), lambda b, c: (b * nc + c, 0))
    return pl.pallas_call(
        _ssd_kernel,
        grid=(bsz, nc),
        in_specs=[tok(d), tok(SSD_D_INNER + SSD_CONV_DIM), const((1, d)), const((d, LANES)),
                  const((1, LANES)), const((1, LANES)), const((SSD_CONV_WIDTH, SSD_CONV_DIM)),
                  const((1, SSD_CONV_DIM)), const((1, SSD_D_INNER)), const((LANES, SSD_D_INNER)),
                  const((1, SSD_D_INNER)), const((SSD_D_INNER, d))],
        out_specs=tok(d),
        out_shape=jax.ShapeDtypeStruct((t, d), F32),
        scratch_shapes=[pltpu.VMEM((SSD_N_GROUPS, SSD_D_STATE, SSD_GROUP_DIM), F32),
                        pltpu.VMEM((SUBLANES, SSD_CONV_DIM), F32)],
        compiler_params=_params("parallel", "arbitrary"),
        name="ssd_scan",
    )(x, zx, g_mix.reshape(1, d), w_dt, _pad_lanes(dt_bias), _pad_lanes(a_log), conv_w.astype(F32),
      conv_b.reshape(1, -1).astype(F32), dskip_e, expand, norm_g.reshape(1, -1).astype(F32),
      w_out.astype(BF16))


HGRN_BLOCK = SUBLANES
HGRN_PAIRS = CHUNK * HGRN_BLOCK
HGRN_HEADS_PER_STEP = 8


def _split3(x):
    x1 = x.astype(BF16)
    r1 = x - x1.astype(F32)
    x2 = r1.astype(BF16)
    x3 = (r1 - x2.astype(F32)).astype(BF16)
    return x1, x2, x3


def _hgrn_kernel(q_ref, f_ref, i_ref, g_ref, lb_ref, gnorm_ref, rep_ref, tile_ref, dif_ref, fold_ref,
                 o_ref, state_scr):
    L, C = CHUNK, HGRN_BLOCK
    NB = L // C
    K = HGRN_HEAD_DIM
    H = state_scr.shape[0]
    W = H * K
    nt = (((1,), (1,)), ((), ()))

    @pl.when(pl.program_id(2) == 0)
    def _():
        state_scr[...] = jnp.zeros_like(state_scr)

    lb = lb_ref[...]
    fl = f_ref[...].astype(F32)
    logf = jnp.log(lb + (1.0 - lb) * _sigmoid(fl))
    kk = (1.0 - lb) * _sigmoid(-fl)
    q = q_ref[...].astype(F32)
    row = lax.broadcasted_iota(jnp.int32, (L, L), 0)
    col = lax.broadcasted_iota(jnp.int32, (L, L), 1)
    b = jnp.dot((row >= col).astype(F32), logf, preferred_element_type=F32, precision=HIGHEST)
    b3 = b.reshape(NB, C, W)
    b_end3 = jnp.broadcast_to(b3[:, C - 1:C, :], b3.shape)
    b_prev3 = jnp.concatenate([jnp.zeros((1, C, W), F32), b_end3[:NB - 1]], axis=0)
    b_end = b_end3.reshape(L, W)
    b_prev = b_prev3.reshape(L, W)
    q_dec = (q * jnp.exp(b - b_prev)).astype(BF16)
    k_dec = (kk * jnp.exp(b_end - b)).astype(BF16)
    blk_dec = jnp.exp(b_end - b_prev)

    v_t = i_ref[...].astype(F32).T
    lane_blk = lax.broadcasted_iota(jnp.int32, (K, L), 1) // C
    inter = []
    for h in range(H):
        cols = slice(h * K, (h + 1) * K)
        v_h, k_h, q_h = v_t[cols], k_dec[:, cols], q_dec[:, cols]
        outer = [jnp.dot(jnp.where(lane_blk == i, v_h, 0.0).astype(BF16), k_h, preferred_element_type=F32)
                 for i in range(NB)]
        s = state_scr[h]
        o_h = jnp.zeros((K, L), F32)
        for i in range(NB):
            r = lax.dot_general(s.astype(BF16), q_h, nt, preferred_element_type=F32)
            o_h = jnp.where(lane_blk == i, r, o_h)
            s = s * blk_dec[i * C:i * C + 1, cols] + outer[i]
        state_scr[h] = s
        inter.append(o_h)

    q_e = jnp.dot(q.T.astype(BF16), rep_ref[...], preferred_element_type=F32)
    k_e = jnp.dot(kk.T.astype(BF16), tile_ref[...], preferred_element_type=F32)
    dif = dif_ref[...]
    seg = sum(jnp.dot(part, dif, preferred_element_type=F32) for part in _split3(b.T))
    j = lax.broadcasted_iota(jnp.int32, (W, HGRN_PAIRS), 1)
    ordered = (j % C) <= ((j // C) % C)
    prod = q_e * k_e * jnp.exp(jnp.where(ordered, seg, -jnp.inf))
    v_e = jnp.dot(v_t.astype(BF16), tile_ref[...], preferred_element_type=F32)
    weighted = [v_e[h * K:(h + 1) * K] * jnp.sum(prod[h * K:(h + 1) * K], axis=0, keepdims=True)
                for h in range(H)]
    o_t = jnp.concatenate(inter, axis=0) + jnp.dot(jnp.concatenate(weighted, axis=0).astype(BF16),
                                                   fold_ref[...], preferred_element_type=F32)
    o = o_t.T
    gate = _silu(g_ref[...].astype(F32))
    gnorm = gnorm_ref[...]
    for h in range(H):
        cols = slice(h * K, (h + 1) * K)
        o_ref[:, cols] = (_rmsnorm(o[:, cols], gnorm) * gate[:, cols]).astype(o_ref.dtype)


def _hgrn_pair_maps():
    c = HGRN_BLOCK
    j = jnp.arange(HGRN_PAIRS)
    t_of = j // c
    s_of = (t_of // c) * c + j % c
    t = jnp.arange(CHUNK)[:, None]
    rep = (t == t_of[None, :]).astype(F32)
    tile = (t == s_of[None, :]).astype(F32)
    return rep.astype(BF16), tile.astype(BF16), (rep - tile).astype(BF16), rep.T.astype(BF16)


def _hgrn_layer(x, bsz, g_mix, w_in, lower_bound, norm_g, w_out):
    t, d = x.shape
    nc = t // bsz // CHUNK
    hd = HGRN_HEAD_DIM
    qfig = _norm_matmul(x, g_mix, w_in.astype(BF16))
    hs = HGRN_HEADS_PER_STEP
    groups = HGRN_N_HEADS // hs
    part = lambda k: pl.BlockSpec((CHUNK, hs * hd), lambda b, h, c: (b * nc + c, k * groups + h))
    const = lambda shape: pl.BlockSpec(shape, lambda b, h, c: (0, 0))
    rep, tile, dif, fold = _hgrn_pair_maps()
    o = pl.pallas_call(
        _hgrn_kernel,
        grid=(bsz, groups, nc),
        in_specs=[part(0), part(1), part(2), part(3),
                  pl.BlockSpec((1, hs * hd), lambda b, h, c: (0, h)), const((1, hd)),
                  const(rep.shape), const(tile.shape), const(dif.shape), const(fold.shape)],
        out_specs=pl.BlockSpec((CHUNK, hs * hd), lambda b, h, c: (b * nc + c, h)),
        out_shape=jax.ShapeDtypeStruct((t, d), BF16),
        scratch_shapes=[pltpu.VMEM((hs, hd, hd), F32)],
        compiler_params=_params("parallel", "parallel", "arbitrary"),
        name="hgrn_scan",
    )(qfig, qfig, qfig, qfig, lower_bound.reshape(1, d).astype(F32), norm_g.reshape(1, hd).astype(F32),
      rep, tile, dif, fold)
    return _matmul_residual(x, o, w_out.astype(BF16))


def _drain(steps):
    try:
        while True:
            next(steps)
    except StopIteration as done:
        return done.value


def _topk_rows_steps(s, k, payload=None):
    r = s.shape[0]
    rows = lax.broadcasted_iota(jnp.int32, s.shape, 0).astype(F32)
    vals, picks = [], []
    for _ in range(k):
        m = jnp.max(s, axis=0, keepdims=True)
        at_max = jnp.where(s == m, rows, float(r))
        pos = jnp.min(at_max, axis=0, keepdims=True)
        hit = at_max == pos
        vals.append(m)
        picks.append(pos if payload is None else jnp.sum(jnp.where(hit, payload, 0.0), axis=0, keepdims=True))
        s = jnp.where(hit, -jnp.inf, s)
        yield
    return jnp.concatenate(vals, axis=0), jnp.concatenate(picks, axis=0)


def _topk_rows(s, k, payload=None):
    return _drain(_topk_rows_steps(s, k, payload))


def _route_head_steps(keys_ref, qt_ref, idx_scr, gate_scr, head):
    top_s, top_i = [], []
    for c in range(2):
        hc = head * 2 + c
        q_hc = qt_ref[pl.ds(pl.multiple_of(hc * PEER_D_HALF, PEER_D_HALF), PEER_D_HALF), :]
        sc = jnp.dot(keys_ref[hc], q_hc, preferred_element_type=F32)
        s, i = yield from _topk_rows_steps(sc, PEER_TOPK)
        top_s.append(s)
        top_i.append(i)
    cand_s = _product_candidates(top_s[0], top_s[1], lambda a, b: a + b)
    cand_i = _product_candidates(top_i[0], top_i[1], lambda a, b: a * float(PEER_N_KEYS) + b)
    best_s, best_i = yield from _topk_rows_steps(cand_s, PEER_TOPK, payload=cand_i)
    e = jnp.exp(best_s - best_s[0:1, :])
    out_rows = pl.ds(pl.multiple_of(head * PEER_TOPK, PEER_TOPK), PEER_TOPK)
    idx_scr[out_rows, :] = best_i
    gate_scr[out_rows, :] = e / jnp.sum(e, axis=0, keepdims=True)


def _product_candidates(first, second, combine):
    half = PEER_TOPK // 2
    parts = [combine(first[0:1, :], second[0:half, :]), combine(first[0:1, :], second[half:, :])]
    parts += [combine(first[a:a + 1, :], second[0:half, :]) for a in range(1, half)]
    parts.append(combine(first[half:, :], second[0:1, :]))
    return jnp.concatenate(parts, axis=0)


def _peer_route_kernel(x_ref, g_ref, wq_ref, keys_ref, h_ref, idx_ref, gate_ref):
    h = _rmsnorm(x_ref[...], g_ref[...])
    h_ref[...] = h
    q_t = lax.dot_general(wq_ref[...], h.astype(BF16), (((1,), (1,)), ((), ())),
                          preferred_element_type=F32)
    idx_rows, gate_rows = [], []
    for head in range(PEER_N_HEADS):
        top_s, top_i = [], []
        for c in range(2):
            hc = head * 2 + c
            q_hc = q_t[hc * PEER_D_HALF:(hc + 1) * PEER_D_HALF, :].astype(BF16)
            sc = jnp.dot(keys_ref[hc], q_hc, preferred_element_type=F32)
            s, i = _topk_rows(sc, PEER_TOPK)
            top_s.append(s)
            top_i.append(i)
        cand_s = _product_candidates(top_s[0], top_s[1], lambda a, b: a + b)
        cand_i = _product_candidates(top_i[0], top_i[1], lambda a, b: a * float(PEER_N_KEYS) + b)
        best_s, best_i = _topk_rows(cand_s, PEER_TOPK, payload=cand_i)
        e = jnp.exp(best_s - best_s[0:1, :])
        gate_rows.append(e / jnp.sum(e, axis=0, keepdims=True))
        idx_rows.append(best_i)
    idx_ref[...] = jnp.concatenate(idx_rows, axis=0).T.astype(jnp.int32) * ROW_SUB
    gate_ref[...] = jnp.concatenate(gate_rows, axis=0).T


def _peer_route(x, g, wq_t, keys, n_tokens, tn=128):
    t, d = x.shape
    t = n_tokens
    nq = wq_t.shape[0]
    return pl.pallas_call(
        _peer_route_kernel,
        grid=(t // tn,),
        in_specs=[pl.BlockSpec((tn, d), lambda i: (i, 0)),
                  pl.BlockSpec((1, d), lambda i: (0, 0)),
                  pl.BlockSpec((nq, d), lambda i: (0, 0)),
                  pl.BlockSpec(keys.shape, lambda i: (0, 0, 0))],
        out_specs=[pl.BlockSpec((tn, d), lambda i: (i, 0)),
                   pl.BlockSpec((tn, PEER_SLOTS), lambda i: (i, 0)),
                   pl.BlockSpec((tn, PEER_SLOTS), lambda i: (i, 0))],
        out_shape=[jax.ShapeDtypeStruct((t, d), F32),
                   jax.ShapeDtypeStruct((t, PEER_SLOTS), jnp.int32),
                   jax.ShapeDtypeStruct((t, PEER_SLOTS), F32)],
        compiler_params=_params("parallel"),
        name="peer_route",
    )(x, g.reshape(1, d), wq_t, keys)


SLAB_ROWS = PEER_SLOTS * ROW_SUB
SLAB_COLS = 2 * SLAB_ROWS
GATHER_UNROLL = 16


def _pack_table(t):
    b = lax.bitcast_convert_type(t.astype(BF16), jnp.uint16).astype(jnp.uint32)
    w = b[:, :ROW_WORDS] | (b[:, ROW_WORDS:] << 16)
    return w.reshape(-1, LANES)


def _slab_maps():
    col = jnp.arange(SLAB_COLS)
    row = jnp.arange(SUBLANES)[:, None]
    diag = (row == ((col // 2) % ROW_SUB + ROW_SUB * (col % 2))[None, :]).astype(F32)
    spread = (jnp.arange(PEER_SLOTS)[:, None] == (col // SUBLANES)[None, :]).astype(BF16)
    return diag, spread


SIDE_EVERY = 32


def _gather_rows(idx_ref, tab_ref, slab, t, side=None):
    base = t * PEER_SLOTS
    for p in range(PEER_SLOTS):
        row = pl.multiple_of(idx_ref[base + p], ROW_SUB)
        slab[p * ROW_SUB:(p + 1) * ROW_SUB, :] = tab_ref[pl.ds(row, ROW_SUB), :]
        if side is not None and p % SIDE_EVERY == SIDE_EVERY - 1:
            next(side, None)


def _pipelined_tokens(tb, gather, finish, slab_a, slab_b, unroll=GATHER_UNROLL, side_work=None):
    gather(slab_a, 0)
    slabs = (slab_a, slab_b)

    def group(i, carry):
        t0 = unroll * i
        side = None if side_work is None else side_work(i)
        for k in range(unroll):
            gather(slabs[(k + 1) % 2], jnp.minimum(t0 + k + 1, tb - 1), side)
            finish(slabs[k % 2], t0 + k)
        if side is not None:
            _drain(side)
        return carry

    lax.fori_loop(0, tb // unroll, group, 0)


_NT = (((1,), (1,)), ((), ()))


def _peer_u_kernel(idx_ref, h_ref, gate_ref, tab_ref, diag_ref, spread_ref, spread_t_ref, crep_all_ref,
                   crep_ref, slab_a, slab_b, rbuf, side_work=None):
    tb = h_ref.shape[0]
    diag = diag_ref[...]

    def finish(slab, t):
        w = pltpu.bitcast(slab[...], BF16)
        r = lax.dot_general(h_ref[t].astype(BF16), w, _NT, preferred_element_type=F32)
        rbuf[pl.ds(t, 1), :] = jnp.sum(r * diag, axis=0, keepdims=True)

    _pipelined_tokens(tb, functools.partial(_gather_rows, idx_ref, tab_ref), finish, slab_a, slab_b,
                      side_work=side_work)
    spread_t = spread_t_ref[...]
    act = sum(jnp.dot(part, spread_t, preferred_element_type=F32) for part in _split3(rbuf[...]))
    gelu = 0.5 * act * (1.0 + lax.erf(act * (2.0 ** -0.5)))
    c = (gate_ref[...] * gelu).astype(BF16)
    crep_ref[...] = jnp.dot(c, spread_ref[...], preferred_element_type=F32)


def _peer_u_route_kernel(idx_ref, h_ref, gate_ref, tab_ref, diag_ref, spread_ref, spread_t_ref, crep_all_ref,
                         x2_ref, g_ref, wq_ref, keys_ref, crep_ref, h2_ref, idx2_ref, gate2_ref,
                         slab_a, slab_b, rbuf, qt_scr, idx_scr, gate_scr):
    h2 = _rmsnorm(x2_ref[...], g_ref[...])
    h2_ref[...] = h2
    qt_scr[...] = lax.dot_general(wq_ref[...], h2.astype(BF16), _NT, preferred_element_type=F32).astype(BF16)
    _peer_u_kernel(idx_ref, h_ref, gate_ref, tab_ref, diag_ref, spread_ref, spread_t_ref, crep_all_ref,
                   crep_ref, slab_a, slab_b, rbuf,
                   side_work=functools.partial(_route_head_steps, keys_ref, qt_scr, idx_scr, gate_scr))
    idx2_ref[...] = idx_scr[...].T.astype(jnp.int32) * ROW_SUB
    gate2_ref[...] = gate_scr[...].T


def _peer_v_kernel(idx_ref, crep_ref, x_ref, tab_ref, diag_ref, o_ref, slab_a, slab_b):
    tb = x_ref.shape[0]
    diag = diag_ref[...]

    def finish(slab, t):
        w = pltpu.bitcast(slab[...], BF16)
        c = (crep_ref[pl.ds(t, 1), :] * diag).astype(BF16)
        y = jnp.dot(c, w, preferred_element_type=F32)
        flat = jnp.concatenate([y[s:s + 1, :] for s in range(SUBLANES)], axis=1)
        o_ref[pl.ds(t, 1), :] = x_ref[pl.ds(t, 1), :] + flat

    _pipelined_tokens(tb, functools.partial(_gather_rows, idx_ref, tab_ref), finish, slab_a, slab_b)


PEER_TB = PEER_N_HEADS * GATHER_UNROLL
PEER_PARTS = 8


def _peer_layer(x, g, w_query, sub_keys, expert_u, expert_v):
    t, d = x.shape
    tb = PEER_TB
    nb = t // tb
    parts = PEER_PARTS if nb % PEER_PARTS == 0 else 2
    npb = nb // parts
    tp = npb * tb
    wq_t = w_query.T.astype(BF16)
    nq = wq_t.shape[0]
    keys = sub_keys.reshape(2 * PEER_N_HEADS, PEER_N_KEYS, PEER_D_HALF).astype(BF16)
    u_packed, v_packed = _pack_table(expert_u), _pack_table(expert_v)
    diag, spread = _slab_maps()
    spread_t = spread.T

    tok = pl.BlockSpec((tb, d), lambda i: (i, 0))
    tok3 = pl.BlockSpec((tb, SUBLANES, LANES), lambda i: (i, 0, 0))
    tok2 = pl.BlockSpec((tb, PEER_SLOTS), lambda i: (i, 0))
    rep2 = pl.BlockSpec((tb, SLAB_COLS), lambda i: (i, 0))
    smem1 = pl.BlockSpec((tb * PEER_SLOTS,), lambda i: (i,), memory_space=pltpu.SMEM)
    table = pl.BlockSpec(u_packed.shape, lambda i: (0, 0), pipeline_mode=pl.Buffered(1))
    const = lambda a: pl.BlockSpec(a.shape, lambda i: (0,) * a.ndim)
    slab = pltpu.VMEM((SLAB_ROWS, LANES), jnp.uint32)
    rbuf = pltpu.VMEM((tb, SLAB_COLS), F32)
    g2 = g.reshape(1, d)

    whole = pl.BlockSpec(memory_space=pl.ANY)
    crep_shape = jax.ShapeDtypeStruct((t, SLAB_COLS), F32)
    part_shapes = [jax.ShapeDtypeStruct((tp, d), F32), jax.ShapeDtypeStruct((tp, PEER_SLOTS), jnp.int32),
                   jax.ShapeDtypeStruct((tp, PEER_SLOTS), F32)]
    u_specs = [smem1, tok3, tok2, table, const(diag), const(spread), const(spread_t), whole]
    crep = jnp.zeros((t, SLAB_COLS), F32)
    h, rows, gate = _peer_route(x, g, wq_t, keys, tp, tn=tb)
    all_rows = [rows]
    for j in range(parts):
        crep_spec = pl.BlockSpec((tb, SLAB_COLS), lambda i, j=j: (i + j * npb, 0))
        u_args = (rows.reshape(-1), h.reshape(tp, SUBLANES, LANES), gate, u_packed, diag, spread, spread_t, crep)
        if j + 1 < parts:
            crep, h, rows, gate = pl.pallas_call(
                _peer_u_route_kernel,
                grid=(npb,),
                in_specs=u_specs + [pl.BlockSpec((tb, d), lambda i, j=j: (i + (j + 1) * npb, 0)),
                                    const(g2), const(wq_t), const(keys)],
                out_specs=[crep_spec, tok, tok2, tok2],
                out_shape=[crep_shape] + part_shapes,
                scratch_shapes=[slab, slab, rbuf, pltpu.VMEM((nq, tb), BF16),
                                pltpu.VMEM((PEER_SLOTS, tb), F32), pltpu.VMEM((PEER_SLOTS, tb), F32)],
                input_output_aliases={len(u_specs) - 1: 0},
                compiler_params=_params("arbitrary"),
                name="peer_u_route",
            )(*u_args, x, g2, wq_t, keys)
            all_rows.append(rows)
        else:
            crep = pl.pallas_call(
                _peer_u_kernel,
                grid=(npb,),
                in_specs=u_specs,
                out_specs=crep_spec,
                out_shape=crep_shape,
                scratch_shapes=[slab, slab, rbuf],
                input_output_aliases={len(u_specs) - 1: 0},
                compiler_params=_params("arbitrary"),
                name="peer_u",
            )(*u_args)
    return pl.pallas_call(
        _peer_v_kernel,
        grid=(nb,),
        in_specs=[smem1, rep2, tok, table, const(diag)],
        out_specs=tok,
        out_shape=jax.ShapeDtypeStruct((t, d), F32),
        scratch_shapes=[slab, slab],
        compiler_params=_params("arbitrary"),
        name="peer_v",
    )(jnp.concatenate(all_rows, axis=0).reshape(-1), crep, x, v_packed, diag)


def _hgrn_lower_bound(lb_logits, layer):
    p = jax.nn.softmax(lb_logits.astype(F32), axis=0)
    return jnp.cumsum(p, axis=0)[layer] - p[0]


def kernel(x, mix_norm_g, ffn_norm_g, final_norm_g, ssd_w_in, ssd_conv_w, ssd_conv_b, ssd_dt_bias, ssd_a_log, ssd_d_skip, ssd_norm_g, ssd_w_out, hgrn_w_in, hgrn_lb_logits, hgrn_norm_g, hgrn_w_out, peer_w_query, peer_sub_keys, peer_u, peer_v):
    bsz, s, d = x.shape
    depth = mix_norm_g.shape[0]
    xf = x.reshape(bsz * s, d)
    for layer in range(depth):
        j = layer // 2
        if layer % 2 == 0:
            xf = _ssd_layer(xf, bsz, mix_norm_g[layer], ssd_w_in[j], ssd_conv_w[j], ssd_conv_b[j],
                            ssd_dt_bias[j], ssd_a_log[j], ssd_d_skip[j], ssd_norm_g[j], ssd_w_out[j])
        else:
            lb = _hgrn_lower_bound(hgrn_lb_logits, layer)
            xf = _hgrn_layer(xf, bsz, mix_norm_g[layer], hgrn_w_in[j], lb, hgrn_norm_g[j], hgrn_w_out[j])
        xf = _peer_layer(xf, ffn_norm_g[layer], peer_w_query[layer], peer_sub_keys[layer],
                         peer_u[layer], peer_v[layer])
    return _final_norm(xf, final_norm_g).reshape(bsz, s, d)
```

```python
import functools

import jax
import jax.numpy as jnp
from jax import lax
from jax.experimental import pallas as pl
from jax.experimental.pallas import tpu as pltpu

F32 = jnp.float32
BF16 = jnp.bfloat16
HIGHEST = lax.Precision.HIGHEST

LANES = 128
SUBLANES = 8
VMEM_LIMIT = 56 * 1024 * 1024

D_MODEL = 1024
NORM_EPS = 1e-6

SSD_D_INNER = 2048
SSD_HEAD_DIM = 64
SSD_N_HEADS = 32
SSD_N_GROUPS = 4
SSD_HEADS_PER_GROUP = 8
SSD_D_STATE = 128
SSD_CONV_WIDTH = 4
SSD_CONV_DIM = SSD_D_INNER + 2 * SSD_N_GROUPS * SSD_D_STATE
SSD_GROUP_DIM = SSD_D_INNER // SSD_N_GROUPS
CHUNK = 128

HGRN_N_HEADS = 8
HGRN_HEAD_DIM = 128

PEER_N_KEYS = 128
PEER_N_HEADS = 8
PEER_D_HALF = 128
PEER_TOPK = 16
PEER_SLOTS = PEER_N_HEADS * PEER_TOPK
ROW_WORDS = D_MODEL // 2
ROW_SUB = ROW_WORDS // LANES


def _rmsnorm(x, g):
    r = lax.rsqrt(jnp.mean(x * x, axis=-1, keepdims=True) + NORM_EPS)
    return (x * r) * g


def _silu(x):
    return x * (1.0 / (1.0 + jnp.exp(-x)))


def _sigmoid(x):
    return 1.0 / (1.0 + jnp.exp(-x))


def _params(*sem):
    return pltpu.CompilerParams(dimension_semantics=sem, vmem_limit_bytes=VMEM_LIMIT)


def _norm_matmul_kernel(x_ref, g_ref, w_ref, o_ref, h_scr):
    @pl.when(pl.program_id(1) == 0)
    def _():
        h_scr[...] = _rmsnorm(x_ref[...], g_ref[...]).astype(BF16)

    o_ref[...] = jnp.dot(h_scr[...], w_ref[...], preferred_element_type=F32).astype(o_ref.dtype)


def _norm_matmul(x, g, w, tm=1024, tn=1024, out_dtype=BF16):
    t, d = x.shape
    n = w.shape[1]
    return pl.pallas_call(
        _norm_matmul_kernel,
        grid=(t // tm, n // tn),
        in_specs=[pl.BlockSpec((tm, d), lambda i, j: (i, 0)),
                  pl.BlockSpec((1, d), lambda i, j: (0, 0)),
                  pl.BlockSpec((d, tn), lambda i, j: (0, j))],
        out_specs=pl.BlockSpec((tm, tn), lambda i, j: (i, j)),
        out_shape=jax.ShapeDtypeStruct((t, n), out_dtype),
        scratch_shapes=[pltpu.VMEM((tm, d), BF16)],
        compiler_params=_params("parallel", "arbitrary"),
        name="norm_matmul",
    )(x, g.reshape(1, d), w)


def _matmul_residual_kernel(x_ref, a_ref, w_ref, o_ref):
    o_ref[...] = x_ref[...] + jnp.dot(a_ref[...], w_ref[...], preferred_element_type=F32)


def _matmul_residual(x, a, w, tm=512):
    t, d = x.shape
    k = a.shape[1]
    return pl.pallas_call(
        _matmul_residual_kernel,
        grid=(t // tm,),
        in_specs=[pl.BlockSpec((tm, d), lambda i: (i, 0)),
                  pl.BlockSpec((tm, k), lambda i: (i, 0)),
                  pl.BlockSpec((k, d), lambda i: (0, 0))],
        out_specs=pl.BlockSpec((tm, d), lambda i: (i, 0)),
        out_shape=jax.ShapeDtypeStruct((t, d), F32),
        compiler_params=_params("parallel"),
        name="matmul_residual",
    )(x, a, w)


def _final_norm_kernel(x_ref, g_ref, o_ref):
    o_ref[...] = _rmsnorm(x_ref[...], g_ref[...])


def _final_norm(x, g, tm=1024):
    t, d = x.shape
    return pl.pallas_call(
        _final_norm_kernel,
        grid=(t // tm,),
        in_specs=[pl.BlockSpec((tm, d), lambda i: (i, 0)),
                  pl.BlockSpec((1, d), lambda i: (0, 0))],
        out_specs=pl.BlockSpec((tm, d), lambda i: (i, 0)),
        out_shape=jax.ShapeDtypeStruct((t, d), F32),
        compiler_params=_params("parallel"),
        name="final_norm",
    )(x, g.reshape(1, d))


def _ssd_kernel(x_ref, zx_ref, gmix_ref, wdt_ref, dtb_ref, alog_ref, convw_ref, convb_ref,
                dskip_ref, expand_ref, gnorm_ref, wout_ref, o_ref, state_scr, tail_scr):
    L = CHUNK
    G, N, P = SSD_N_GROUPS, SSD_D_STATE, SSD_HEAD_DIM
    GD = SSD_GROUP_DIM

    @pl.when(pl.program_id(1) == 0)
    def _():
        state_scr[...] = jnp.zeros_like(state_scr)
        tail_scr[...] = jnp.zeros_like(tail_scr)

    x = x_ref[...]
    h = _rmsnorm(x, gmix_ref[...]).astype(BF16)
    dt_raw = jnp.dot(h, wdt_ref[...], preferred_element_type=F32) + dtb_ref[...]
    dt = jnp.maximum(dt_raw, 0.0) + jnp.log(1.0 + jnp.exp(-jnp.abs(dt_raw)))
    a = -jnp.exp(alog_ref[...])
    row = lax.broadcasted_iota(jnp.int32, (L, L), 0)
    col = lax.broadcasted_iota(jnp.int32, (L, L), 1)
    causal = row >= col
    tril = causal.astype(F32)
    acs = jnp.dot(tril, dt * a, preferred_element_type=F32, precision=HIGHEST)
    acs_t = acs.T
    dt_t = dt.T
    exp_acs = jnp.exp(acs)
    tail = jnp.exp(acs[L - 1:L, :] - acs) * dt
    both = jnp.concatenate([exp_acs, tail], axis=0)
    both_e = jnp.dot(both, expand_ref[...], preferred_element_type=F32, precision=HIGHEST)
    exp_acs_e = both_e[:L]
    tail_e = both_e[L:]

    zx = zx_ref[...]
    z = zx[:, :SSD_D_INNER].astype(F32)
    xbc = zx[:, SSD_D_INNER:].astype(F32)
    ext = jnp.concatenate([tail_scr[...], xbc], axis=0)
    tail_scr[...] = xbc[L - SUBLANES:, :]
    conv = convb_ref[...] + convw_ref[SSD_CONV_WIDTH - 1:SSD_CONV_WIDTH, :] * xbc
    for k in range(SSD_CONV_WIDTH - 1):
        off = SUBLANES - (SSD_CONV_WIDTH - 1) + k
        conv = conv + convw_ref[k:k + 1, :] * ext[off:off + L, :]
    xbc = _silu(conv)
    xs = xbc[:, :SSD_D_INNER]
    bm = xbc[:, SSD_D_INNER:SSD_D_INNER + G * N]
    cm = xbc[:, SSD_D_INNER + G * N:]

    xs_tail = (xs * tail_e).astype(BF16)
    xs_b = xs.astype(BF16)
    y_parts = []
    for g in range(G):
        b_g = bm[:, g * N:(g + 1) * N].astype(BF16)
        c_g = cm[:, g * N:(g + 1) * N].astype(BF16)
        cb = lax.dot_general(c_g, b_g, (((1,), (1,)), ((), ())), preferred_element_type=F32)
        st = state_scr[g]
        y_inter = jnp.dot(c_g, st.astype(BF16), preferred_element_type=F32)
        y_g = y_inter * exp_acs_e[:, g * GD:(g + 1) * GD]
        intra = []
        for j in range(SSD_HEADS_PER_GROUP):
            hh = g * SSD_HEADS_PER_GROUP + j
            seg = acs[:, hh:hh + 1] - acs_t[hh:hh + 1, :]
            decay = jnp.exp(jnp.where(causal, seg, -jnp.inf))
            w = (cb * decay * dt_t[hh:hh + 1, :]).astype(BF16)
            intra.append(jnp.dot(w, xs_b[:, hh * P:(hh + 1) * P], preferred_element_type=F32))
        y_g = y_g + jnp.concatenate(intra, axis=1)
        y_parts.append(y_g)
        upd = lax.dot_general(b_g, xs_tail[:, g * GD:(g + 1) * GD], (((0,), (0,)), ((), ())),
                              preferred_element_type=F32)
        state_scr[g] = st * exp_acs_e[L - 1:L, g * GD:(g + 1) * GD] + upd
    y = jnp.concatenate(y_parts, axis=1)
    y = y + xs * dskip_ref[...]
    y = y * _silu(z)
    normed = []
    for g in range(G):
        y_g = y[:, g * GD:(g + 1) * GD]
        normed.append(_rmsnorm(y_g, gnorm_ref[:, g * GD:(g + 1) * GD]))
    yn = jnp.concatenate(normed, axis=1).astype(BF16)
    o_ref[...] = x + jnp.dot(yn, wout_ref[...], preferred_element_type=F32)


def _pad_lanes(v):
    return jnp.pad(v.astype(F32), (0, LANES - v.shape[0])).reshape(1, LANES)


def _ssd_layer(x, bsz, g_mix, w_in, conv_w, conv_b, dt_bias, a_log, d_skip, norm_g, w_out):
    t, d = x.shape
    nc = t // bsz // CHUNK
    w_zx = w_in[:, :SSD_D_INNER + SSD_CONV_DIM].astype(BF16)
    w_dt = jnp.pad(w_in[:, SSD_D_INNER + SSD_CONV_DIM:], ((0, 0), (0, LANES - SSD_N_HEADS))).astype(BF16)
    zx = _norm_matmul(x, g_mix, w_zx)
    head_of_lane = jnp.arange(SSD_D_INNER) // SSD_HEAD_DIM
    expand = (jnp.arange(LANES)[:, None] == head_of_lane[None, :]).astype(F32)
    dskip_e = jnp.repeat(d_skip.astype(F32), SSD_HEAD_DIM).reshape(1, SSD_D_INNER)
    const = lambda shape: pl.BlockSpec(shape, lambda b, c: (0,) * len(shape))
    tok = lambda width: pl.BlockSpec((CHUNK, width), lambda b, c: (b * nc + c, 0))
    return pl.pallas_call(
        _ssd_kernel,
        grid=(bsz, nc),
        in_specs=[tok(d), tok(SSD_D_INNER + SSD_CONV_DIM), const((1, d)), const((d, LANES)),
                  const((1, LANES)), const((1, LANES)), const((SSD_CONV_WIDTH, SSD_CONV_DIM)),
                  const((1, SSD_CONV_DIM)), const((1, SSD_D_INNER)), const((LANES, SSD_D_INNER)),
                  const((1, SSD_D_INNER)), const((SSD_D_INNER, d))],
        out_specs=tok(d),
        out_shape=jax.ShapeDtypeStruct((t, d), F32),
        scratch_shapes=[pltpu.VMEM((SSD_N_GROUPS, SSD_D_STATE, SSD_GROUP_DIM), F32),
                        pltpu.VMEM((SUBLANES, SSD_CONV_DIM), F32)],
        compiler_params=_params("parallel", "arbitrary"),
        name="ssd_scan",
    )(x, zx, g_mix.reshape(1, d), w_dt, _pad_lanes(dt_bias), _pad_lanes(a_log), conv_w.astype(F32),
      conv_b.reshape(1, -1).astype(F32), dskip_e, expand, norm_g.reshape(1, -1).astype(F32),
      w_out.astype(BF16))


HGRN_BLOCK = SUBLANES
HGRN_PAIRS = CHUNK * HGRN_BLOCK
HGRN_HEADS_PER_STEP = 8


def _split3(x):
    x1 = x.astype(BF16)
    r1 = x - x1.astype(F32)
    x2 = r1.astype(BF16)
    x3 = (r1 - x2.astype(F32)).astype(BF16)
    return x1, x2, x3


def _hgrn_kernel(q_ref, f_ref, i_ref, g_ref, lb_ref, gnorm_ref, rep_ref, tile_ref, dif_ref, fold_ref,
                 o_ref, state_scr):
    L, C = CHUNK, HGRN_BLOCK
    NB = L // C
    K = HGRN_HEAD_DIM
    H = state_scr.shape[0]
    W = H * K
    nt = (((1,), (1,)), ((), ()))

    @pl.when(pl.program_id(2) == 0)
    def _():
        state_scr[...] = jnp.zeros_like(state_scr)

    lb = lb_ref[...]
    fl = f_ref[...].astype(F32)
    logf = jnp.log(lb + (1.0 - lb) * _sigmoid(fl))
    kk = (1.0 - lb) * _sigmoid(-fl)
    q = q_ref[...].astype(F32)
    row = lax.broadcasted_iota(jnp.int32, (L, L), 0)
    col = lax.broadcasted_iota(jnp.int32, (L, L), 1)
    b = jnp.dot((row >= col).astype(F32), logf, preferred_element_type=F32, precision=HIGHEST)
    b3 = b.reshape(NB, C, W)
    b_end3 = jnp.broadcast_to(b3[:, C - 1:C, :], b3.shape)
    b_prev3 = jnp.concatenate([jnp.zeros((1, C, W), F32), b_end3[:NB - 1]], axis=0)
    b_end = b_end3.reshape(L, W)
    b_prev = b_prev3.reshape(L, W)
    q_dec = (q * jnp.exp(b - b_prev)).astype(BF16)
    k_dec = (kk * jnp.exp(b_end - b)).astype(BF16)
    blk_dec = jnp.exp(b_end - b_prev)

    v_t = i_ref[...].astype(F32).T
    lane_blk = lax.broadcasted_iota(jnp.int32, (K, L), 1) // C
    inter = []
    for h in range(H):
        cols = slice(h * K, (h + 1) * K)
        v_h, k_h, q_h = v_t[cols], k_dec[:, cols], q_dec[:, cols]
        outer = [jnp.dot(jnp.where(lane_blk == i, v_h, 0.0).astype(BF16), k_h, preferred_element_type=F32)
                 for i in range(NB)]
        s = state_scr[h]
        o_h = jnp.zeros((K, L), F32)
        for i in range(NB):
            r = lax.dot_general(s.astype(BF16), q_h, nt, preferred_element_type=F32)
            o_h = jnp.where(lane_blk == i, r, o_h)
            s = s * blk_dec[i * C:i * C + 1, cols] + outer[i]
        state_scr[h] = s
        inter.append(o_h)

    q_e = jnp.dot(q.T.astype(BF16), rep_ref[...], preferred_element_type=F32)
    k_e = jnp.dot(kk.T.astype(BF16), tile_ref[...], preferred_element_type=F32)
    dif = dif_ref[...]
    seg = sum(jnp.dot(part, dif, preferred_element_type=F32) for part in _split3(b.T))
    j = lax.broadcasted_iota(jnp.int32, (W, HGRN_PAIRS), 1)
    ordered = (j % C) <= ((j // C) % C)
    prod = q_e * k_e * jnp.exp(jnp.where(ordered, seg, -jnp.inf))
    v_e = jnp.dot(v_t.astype(BF16), tile_ref[...], preferred_element_type=F32)
    weighted = [v_e[h * K:(h + 1) * K] * jnp.sum(prod[h * K:(h + 1) * K], axis=0, keepdims=True)
                for h in range(H)]
    o_t = jnp.concatenate(inter, axis=0) + jnp.dot(jnp.concatenate(weighted, axis=0).astype(BF16),
                                                   fold_ref[...], preferred_element_type=F32)
    o = o_t.T
    gate = _silu(g_ref[...].astype(F32))
    gnorm = gnorm_ref[...]
    for h in range(H):
        cols = slice(h * K, (h + 1) * K)
        o_ref[:, cols] = (_rmsnorm(o[:, cols], gnorm) * gate[:, cols]).astype(o_ref.dtype)


def _hgrn_pair_maps():
    c = HGRN_BLOCK
    j = jnp.arange(HGRN_PAIRS)
    t_of = j // c
    s_of = (t_of // c) * c + j % c
    t = jnp.arange(CHUNK)[:, None]
    rep = (t == t_of[None, :]).astype(F32)
    tile = (t == s_of[None, :]).astype(F32)
    return rep.astype(BF16), tile.astype(BF16), (rep - tile).astype(BF16), rep.T.astype(BF16)


def _hgrn_layer(x, bsz, g_mix, w_in, lower_bound, norm_g, w_out):
    t, d = x.shape
    nc = t // bsz // CHUNK
    hd = HGRN_HEAD_DIM
    qfig = _norm_matmul(x, g_mix, w_in.astype(BF16))
    hs = HGRN_HEADS_PER_STEP
    groups = HGRN_N_HEADS // hs
    part = lambda k: pl.BlockSpec((CHUNK, hs * hd), lambda b, h, c: (b * nc + c, k * groups + h))
    const = lambda shape: pl.BlockSpec(shape, lambda b, h, c: (0, 0))
    rep, tile, dif, fold = _hgrn_pair_maps()
    o = pl.pallas_call(
        _hgrn_kernel,
        grid=(bsz, groups, nc),
        in_specs=[part(0), part(1), part(2), part(3),
                  pl.BlockSpec((1, hs * hd), lambda b, h, c: (0, h)), const((1, hd)),
                  const(rep.shape), const(tile.shape), const(dif.shape), const(fold.shape)],
        out_specs=pl.BlockSpec((CHUNK, hs * hd), lambda b, h, c: (b * nc + c, h)),
        out_shape=jax.ShapeDtypeStruct((t, d), BF16),
        scratch_shapes=[pltpu.VMEM((hs, hd, hd), F32)],
        compiler_params=_params("parallel", "parallel", "arbitrary"),
        name="hgrn_scan",
    )(qfig, qfig, qfig, qfig, lower_bound.reshape(1, d).astype(F32), norm_g.reshape(1, hd).astype(F32),
      rep, tile, dif, fold)
    return _matmul_residual(x, o, w_out.astype(BF16))


def _drain(steps):
    try:
        while True:
            next(steps)
    except StopIteration as done:
        return done.value


def _topk_rows_steps(s, k, payload=None):
    r = s.shape[0]
    rows = lax.broadcasted_iota(jnp.int32, s.shape, 0).astype(F32)
    vals, picks = [], []
    for _ in range(k):
        m = jnp.max(s, axis=0, keepdims=True)
        at_max = jnp.where(s == m, rows, float(r))
        pos = jnp.min(at_max, axis=0, keepdims=True)
        hit = at_max == pos
        vals.append(m)
        picks.append(pos if payload is None else jnp.sum(jnp.where(hit, payload, 0.0), axis=0, keepdims=True))
        s = jnp.where(hit, -jnp.inf, s)
        yield
    return jnp.concatenate(vals, axis=0), jnp.concatenate(picks, axis=0)


def _topk_rows(s, k, payload=None):
    return _drain(_topk_rows_steps(s, k, payload))


def _route_head_steps(keys_ref, qt_ref, idx_scr, gate_scr, head):
    top_s, top_i = [], []
    for c in range(2):
        hc = head * 2 + c
        q_hc = qt_ref[pl.ds(pl.multiple_of(hc * PEER_D_HALF, PEER_D_HALF), PEER_D_HALF), :]
        sc = jnp.dot(keys_ref[hc], q_hc, preferred_element_type=F32)
        s, i = yield from _topk_rows_steps(sc, PEER_TOPK)
        top_s.append(s)
        top_i.append(i)
    cand_s = _product_candidates(top_s[0], top_s[1], lambda a, b: a + b)
    cand_i = _product_candidates(top_i[0], top_i[1], lambda a, b: a * float(PEER_N_KEYS) + b)
    best_s, best_i = yield from _topk_rows_steps(cand_s, PEER_TOPK, payload=cand_i)
    e = jnp.exp(best_s - best_s[0:1, :])
    out_rows = pl.ds(pl.multiple_of(head * PEER_TOPK, PEER_TOPK), PEER_TOPK)
    idx_scr[out_rows, :] = best_i
    gate_scr[out_rows, :] = e / jnp.sum(e, axis=0, keepdims=True)


def _product_candidates(first, second, combine):
    half = PEER_TOPK // 2
    parts = [combine(first[0:1, :], second[0:half, :]), combine(first[0:1, :], second[half:, :])]
    parts += [combine(first[a:a + 1, :], second[0:half, :]) for a in range(1, half)]
    parts.append(combine(first[half:, :], second[0:1, :]))
    return jnp.concatenate(parts, axis=0)


def _peer_route_kernel(x_ref, g_ref, wq_ref, keys_ref, h_ref, idx_ref, gate_ref):
    h = _rmsnorm(x_ref[...], g_ref[...])
    h_ref[...] = h
    q_t = lax.dot_general(wq_ref[...], h.astype(BF16), (((1,), (1,)), ((), ())),
                          preferred_element_type=F32)
    idx_rows, gate_rows = [], []
    for head in range(PEER_N_HEADS):
        top_s, top_i = [], []
        for c in range(2):
            hc = head * 2 + c
            q_hc = q_t[hc * PEER_D_HALF:(hc + 1) * PEER_D_HALF, :].astype(BF16)
            sc = jnp.dot(keys_ref[hc], q_hc, preferred_element_type=F32)
            s, i = _topk_rows(sc, PEER_TOPK)
            top_s.append(s)
            top_i.append(i)
        cand_s = _product_candidates(top_s[0], top_s[1], lambda a, b: a + b)
        cand_i = _product_candidates(top_i[0], top_i[1], lambda a, b: a * float(PEER_N_KEYS) + b)
        best_s, best_i = _topk_rows(cand_s, PEER_TOPK, payload=cand_i)
        e = jnp.exp(best_s - best_s[0:1, :])
        gate_rows.append(e / jnp.sum(e, axis=0, keepdims=True))
        idx_rows.append(best_i)
    idx_ref[...] = jnp.concatenate(idx_rows, axis=0).T.astype(jnp.int32) * ROW_SUB
    gate_ref[...] = jnp.concatenate(gate_rows, axis=0).T


def _peer_route(x, g, wq_t, keys, n_tokens, tn=128):
    t, d = x.shape
    t = n_tokens
    nq = wq_t.shape[0]
    return pl.pallas_call(
        _peer_route_kernel,
        grid=(t // tn,),
        in_specs=[pl.BlockSpec((tn, d), lambda i: (i, 0)),
                  pl.BlockSpec((1, d), lambda i: (0, 0)),
                  pl.BlockSpec((nq, d), lambda i: (0, 0)),
                  pl.BlockSpec(keys.shape, lambda i: (0, 0, 0))],
        out_specs=[pl.BlockSpec((tn, d), lambda i: (i, 0)),
                   pl.BlockSpec((tn, PEER_SLOTS), lambda i: (i, 0)),
                   pl.BlockSpec((tn, PEER_SLOTS), lambda i: (i, 0))],
        out_shape=[jax.ShapeDtypeStruct((t, d), F32),
                   jax.ShapeDtypeStruct((t, PEER_SLOTS), jnp.int32),
                   jax.ShapeDtypeStruct((t, PEER_SLOTS), F32)],
        compiler_params=_params("parallel"),
        name="peer_route",
    )(x, g.reshape(1, d), wq_t, keys)


SLAB_ROWS = PEER_SLOTS * ROW_SUB
SLAB_COLS = 2 * SLAB_ROWS
GATHER_UNROLL = 16


def _pack_table(t):
    b = lax.bitcast_convert_type(t.astype(BF16), jnp.uint16).astype(jnp.uint32)
    w = b[:, :ROW_WORDS] | (b[:, ROW_WORDS:] << 16)
    return w.reshape(-1, LANES)


def _slab_maps():
    col = jnp.arange(SLAB_COLS)
    row = jnp.arange(SUBLANES)[:, None]
    diag = (row == ((col // 2) % ROW_SUB + ROW_SUB * (col % 2))[None, :]).astype(F32)
    spread = (jnp.arange(PEER_SLOTS)[:, None] == (col // SUBLANES)[None, :]).astype(BF16)
    return diag, spread


SIDE_EVERY = 32


def _gather_rows(idx_ref, tab_ref, slab, t, side=None):
    base = t * PEER_SLOTS
    for p in range(PEER_SLOTS):
        row = pl.multiple_of(idx_ref[base + p], ROW_SUB)
        slab[p * ROW_SUB:(p + 1) * ROW_SUB, :] = tab_ref[pl.ds(row, ROW_SUB), :]
        if side is not None and p % SIDE_EVERY == SIDE_EVERY - 1:
            next(side, None)


def _pipelined_tokens(tb, gather, finish, slab_a, slab_b, unroll=GATHER_UNROLL, side_work=None):
    gather(slab_a, 0)
    slabs = (slab_a, slab_b)

    def group(i, carry):
        t0 = unroll * i
        side = None if side_work is None else side_work(i)
        for k in range(unroll):
            gather(slabs[(k + 1) % 2], jnp.minimum(t0 + k + 1, tb - 1), side)
            finish(slabs[k % 2], t0 + k)
        if side is not None:
            _drain(side)
        return carry

    lax.fori_loop(0, tb // unroll, group, 0)


_NT = (((1,), (1,)), ((), ()))


def _peer_u_kernel(idx_ref, h_ref, gate_ref, tab_ref, diag_ref, spread_ref, spread_t_ref, crep_all_ref,
                   crep_ref, slab_a, slab_b, rbuf, side_work=None):
    tb = h_ref.shape[0]
    diag = diag_ref[...]

    def finish(slab, t):
        w = pltpu.bitcast(slab[...], BF16)
        r = lax.dot_general(h_ref[t].astype(BF16), w, _NT, preferred_element_type=F32)
        rbuf[pl.ds(t, 1), :] = jnp.sum(r * diag, axis=0, keepdims=True)

    _pipelined_tokens(tb, functools.partial(_gather_rows, idx_ref, tab_ref), finish, slab_a, slab_b,
                      side_work=side_work)
    spread_t = spread_t_ref[...]
    act = sum(jnp.dot(part, spread_t, preferred_element_type=F32) for part in _split3(rbuf[...]))
    gelu = 0.5 * act * (1.0 + lax.erf(act * (2.0 ** -0.5)))
    c = (gate_ref[...] * gelu).astype(BF16)
    crep_ref[...] = jnp.dot(c, spread_ref[...], preferred_element_type=F32)


def _peer_u_route_kernel(idx_ref, h_ref, gate_ref, tab_ref, diag_ref, spread_ref, spread_t_ref, crep_all_ref,
                         x2_ref, g_ref, wq_ref, keys_ref, crep_ref, h2_ref, idx2_ref, gate2_ref,
                         slab_a, slab_b, rbuf, qt_scr, idx_scr, gate_scr):
    h2 = _rmsnorm(x2_ref[...], g_ref[...])
    h2_ref[...] = h2
    qt_scr[...] = lax.dot_general(wq_ref[...], h2.astype(BF16), _NT, preferred_element_type=F32).astype(BF16)
    _peer_u_kernel(idx_ref, h_ref, gate_ref, tab_ref, diag_ref, spread_ref, spread_t_ref, crep_all_ref,
                   crep_ref, slab_a, slab_b, rbuf,
                   side_work=functools.partial(_route_head_steps, keys_ref, qt_scr, idx_scr, gate_scr))
    idx2_ref[...] = idx_scr[...].T.astype(jnp.int32) * ROW_SUB
    gate2_ref[...] = gate_scr[...].T


def _peer_v_kernel(idx_ref, crep_ref, x_ref, tab_ref, diag_ref, outg_ref, o_ref, slab_a, slab_b, *, out_norm):
    tb = x_ref.shape[0]
    diag = diag_ref[...]

    def finish(slab, t):
        w = pltpu.bitcast(slab[...], BF16)
        c = (crep_ref[pl.ds(t, 1), :] * diag).astype(BF16)
        y = jnp.dot(c, w, preferred_element_type=F32)
        flat = jnp.concatenate([y[s:s + 1, :] for s in range(SUBLANES)], axis=1)
        o_ref[pl.ds(t, 1), :] = x_ref[pl.ds(t, 1), :] + flat

    _pipelined_tokens(tb, functools.partial(_gather_rows, idx_ref, tab_ref), finish, slab_a, slab_b)
    if out_norm:
        o_ref[...] = _rmsnorm(o_ref[...], outg_ref[...])


PEER_TB = PEER_N_HEADS * GATHER_UNROLL
PEER_PARTS = 8


def _peer_layer(x, g, w_query, sub_keys, expert_u, expert_v, out_norm_g=None):
    t, d = x.shape
    tb = PEER_TB
    nb = t // tb
    parts = PEER_PARTS if nb % PEER_PARTS == 0 else 2
    npb = nb // parts
    tp = npb * tb
    wq_t = w_query.T.astype(BF16)
    nq = wq_t.shape[0]
    keys = sub_keys.reshape(2 * PEER_N_HEADS, PEER_N_KEYS, PEER_D_HALF).astype(BF16)
    u_packed, v_packed = _pack_table(expert_u), _pack_table(expert_v)
    diag, spread = _slab_maps()
    spread_t = spread.T

    tok = pl.BlockSpec((tb, d), lambda i: (i, 0))
    tok3 = pl.BlockSpec((tb, SUBLANES, LANES), lambda i: (i, 0, 0))
    tok2 = pl.BlockSpec((tb, PEER_SLOTS), lambda i: (i, 0))
    rep2 = pl.BlockSpec((tb, SLAB_COLS), lambda i: (i, 0))
    smem1 = pl.BlockSpec((tb * PEER_SLOTS,), lambda i: (i,), memory_space=pltpu.SMEM)
    table = pl.BlockSpec(u_packed.shape, lambda i: (0, 0), pipeline_mode=pl.Buffered(1))
    const = lambda a: pl.BlockSpec(a.shape, lambda i: (0,) * a.ndim)
    slab = pltpu.VMEM((SLAB_ROWS, LANES), jnp.uint32)
    rbuf = pltpu.VMEM((tb, SLAB_COLS), F32)
    g2 = g.reshape(1, d)

    whole = pl.BlockSpec(memory_space=pl.ANY)
    crep_shape = jax.ShapeDtypeStruct((t, SLAB_COLS), F32)
    part_shapes = [jax.ShapeDtypeStruct((tp, d), F32), jax.ShapeDtypeStruct((tp, PEER_SLOTS), jnp.int32),
                   jax.ShapeDtypeStruct((tp, PEER_SLOTS), F32)]
    u_specs = [smem1, tok3, tok2, table, const(diag), const(spread), const(spread_t), whole]
    crep = jnp.zeros((t, SLAB_COLS), F32)
    h, rows, gate = _peer_route(x, g, wq_t, keys, tp, tn=tb)
    all_rows = [rows]
    for j in range(parts):
        crep_spec = pl.BlockSpec((tb, SLAB_COLS), lambda i, j=j: (i + j * npb, 0))
        u_args = (rows.reshape(-1), h.reshape(tp, SUBLANES, LANES), gate, u_packed, diag, spread, spread_t, crep)
        if j + 1 < parts:
            crep, h, rows, gate = pl.pallas_call(
                _peer_u_route_kernel,
                grid=(npb,),
                in_specs=u_specs + [pl.BlockSpec((tb, d), lambda i, j=j: (i + (j + 1) * npb, 0)),
                                    const(g2), const(wq_t), const(keys)],
                out_specs=[crep_spec, tok, tok2, tok2],
                out_shape=[crep_shape] + part_shapes,
                scratch_shapes=[slab, slab, rbuf, pltpu.VMEM((nq, tb), BF16),
                                pltpu.VMEM((PEER_SLOTS, tb), F32), pltpu.VMEM((PEER_SLOTS, tb), F32)],
                input_output_aliases={len(u_specs) - 1: 0},
                compiler_params=_params("arbitrary"),
                name="peer_u_route",
            )(*u_args, x, g2, wq_t, keys)
            all_rows.append(rows)
        else:
            crep = pl.pallas_call(
                _peer_u_kernel,
                grid=(npb,),
                in_specs=u_specs,
                out_specs=crep_spec,
                out_shape=crep_shape,
                scratch_shapes=[slab, slab, rbuf],
                input_output_aliases={len(u_specs) - 1: 0},
                compiler_params=_params("arbitrary"),
                name="peer_u",
            )(*u_args)
    out_g = (g if out_norm_g is None else out_norm_g).reshape(1, d)
    return pl.pallas_call(
        functools.partial(_peer_v_kernel, out_norm=out_norm_g is not None),
        grid=(nb,),
        in_specs=[smem1, rep2, tok, table, const(diag), const(out_g)],
        out_specs=tok,
        out_shape=jax.ShapeDtypeStruct((t, d), F32),
        scratch_shapes=[slab, slab],
        compiler_params=_params("arbitrary"),
        name="peer_v",
    )(jnp.concatenate(all_rows, axis=0).reshape(-1), crep, x, v_packed, diag, out_g)


def _hgrn_lower_bound(lb_logits, layer):
    p = jax.nn.softmax(lb_logits.astype(F32), axis=0)
    return jnp.cumsum(p, axis=0)[layer] - p[0]


def kernel(x, mix_norm_g, ffn_norm_g, final_norm_g, ssd_w_in, ssd_conv_w, ssd_conv_b, ssd_dt_bias, ssd_a_log, ssd_d_skip, ssd_norm_g, ssd_w_out, hgrn_w_in, hgrn_lb_logits, hgrn_norm_g, hgrn_w_out, peer_w_query, peer_sub_keys, peer_u, peer_v):
    bsz, s, d = x.shape
    depth = mix_norm_g.shape[0]
    xf = x.reshape(bsz * s, d)
    for layer in range(depth):
        j = layer // 2
        if layer % 2 == 0:
            xf = _ssd_layer(xf, bsz, mix_norm_g[layer], ssd_w_in[j], ssd_conv_w[j], ssd_conv_b[j],
                            ssd_dt_bias[j], ssd_a_log[j], ssd_d_skip[j], ssd_norm_g[j], ssd_w_out[j])
        else:
            lb = _hgrn_lower_bound(hgrn_lb_logits, layer)
            xf = _hgrn_layer(xf, bsz, mix_norm_g[layer], hgrn_w_in[j], lb, hgrn_norm_g[j], hgrn_w_out[j])
        xf = _peer_layer(xf, ffn_norm_g[layer], peer_w_query[layer], peer_sub_keys[layer],
                         peer_u[layer], peer_v[layer],
                         out_norm_g=final_norm_g if layer == depth - 1 else None)
    return xf.reshape(bsz, s, d)
```

```python
import functools

import jax
import jax.numpy as jnp
from jax import lax
from jax.experimental import pallas as pl
from jax.experimental.pallas import tpu as pltpu

F32 = jnp.float32
BF16 = jnp.bfloat16
HIGHEST = lax.Precision.HIGHEST

LANES = 128
SUBLANES = 8
VMEM_LIMIT = 56 * 1024 * 1024

D_MODEL = 1024
NORM_EPS = 1e-6

SSD_D_INNER = 2048
SSD_HEAD_DIM = 64
SSD_N_HEADS = 32
SSD_N_GROUPS = 4
SSD_HEADS_PER_GROUP = 8
SSD_D_STATE = 128
SSD_CONV_WIDTH = 4
SSD_CONV_DIM = SSD_D_INNER + 2 * SSD_N_GROUPS * SSD_D_STATE
SSD_GROUP_DIM = SSD_D_INNER // SSD_N_GROUPS
CHUNK = 128

HGRN_N_HEADS = 8
HGRN_HEAD_DIM = 128

PEER_N_KEYS = 128
PEER_N_HEADS = 8
PEER_D_HALF = 128
PEER_TOPK = 16
PEER_SLOTS = PEER_N_HEADS * PEER_TOPK
ROW_WORDS = D_MODEL // 2
ROW_SUB = ROW_WORDS // LANES


def _rmsnorm(x, g):
    r = lax.rsqrt(jnp.mean(x * x, axis=-1, keepdims=True) + NORM_EPS)
    return (x * r) * g


def _silu(x):
    return x * (1.0 / (1.0 + jnp.exp(-x)))


def _sigmoid(x):
    return 1.0 / (1.0 + jnp.exp(-x))


def _params(*sem):
    return pltpu.CompilerParams(dimension_semantics=sem, vmem_limit_bytes=VMEM_LIMIT)


def _norm_matmul_kernel(x_ref, g_ref, w_ref, o_ref, h_scr):
    @pl.when(pl.program_id(1) == 0)
    def _():
        h_scr[...] = _rmsnorm(x_ref[...], g_ref[...]).astype(BF16)

    o_ref[...] = jnp.dot(h_scr[...], w_ref[...], preferred_element_type=F32).astype(o_ref.dtype)


def _norm_matmul(x, g, w, tm=1024, tn=1024, out_dtype=BF16):
    t, d = x.shape
    n = w.shape[1]
    return pl.pallas_call(
        _norm_matmul_kernel,
        grid=(t // tm, n // tn),
        in_specs=[pl.BlockSpec((tm, d), lambda i, j: (i, 0)),
                  pl.BlockSpec((1, d), lambda i, j: (0, 0)),
                  pl.BlockSpec((d, tn), lambda i, j: (0, j))],
        out_specs=pl.BlockSpec((tm, tn), lambda i, j: (i, j)),
        out_shape=jax.ShapeDtypeStruct((t, n), out_dtype),
        scratch_shapes=[pltpu.VMEM((tm, d), BF16)],
        compiler_params=_params("parallel", "arbitrary"),
        name="norm_matmul",
    )(x, g.reshape(1, d), w)


def _matmul_residual_kernel(x_ref, a_ref, w_ref, o_ref):
    o_ref[...] = x_ref[...] + jnp.dot(a_ref[...], w_ref[...], preferred_element_type=F32)


def _matmul_residual(x, a, w, tm=512):
    t, d = x.shape
    k = a.shape[1]
    return pl.pallas_call(
        _matmul_residual_kernel,
        grid=(t // tm,),
        in_specs=[pl.BlockSpec((tm, d), lambda i: (i, 0)),
                  pl.BlockSpec((tm, k), lambda i: (i, 0)),
                  pl.BlockSpec((k, d), lambda i: (0, 0))],
        out_specs=pl.BlockSpec((tm, d), lambda i: (i, 0)),
        out_shape=jax.ShapeDtypeStruct((t, d), F32),
        compiler_params=_params("parallel"),
        name="matmul_residual",
    )(x, a, w)


def _final_norm_kernel(x_ref, g_ref, o_ref):
    o_ref[...] = _rmsnorm(x_ref[...], g_ref[...])


def _final_norm(x, g, tm=1024):
    t, d = x.shape
    return pl.pallas_call(
        _final_norm_kernel,
        grid=(t // tm,),
        in_specs=[pl.BlockSpec((tm, d), lambda i: (i, 0)),
                  pl.BlockSpec((1, d), lambda i: (0, 0))],
        out_specs=pl.BlockSpec((tm, d), lambda i: (i, 0)),
        out_shape=jax.ShapeDtypeStruct((t, d), F32),
        compiler_params=_params("parallel"),
        name="final_norm",
    )(x, g.reshape(1, d))


def _ssd_kernel(x_ref, zx_ref, gmix_ref, wdt_ref, dtb_ref, alog_ref, convw_ref, convb_ref,
                dskip_ref, expand_ref, gnorm_ref, wout_ref, o_ref, state_scr, tail_scr):
    L = CHUNK
    G, N, P = SSD_N_GROUPS, SSD_D_STATE, SSD_HEAD_DIM
    GD = SSD_GROUP_DIM

    @pl.when(pl.program_id(1) == 0)
    def _():
        state_scr[...] = jnp.zeros_like(state_scr)
        tail_scr[...] = jnp.zeros_like(tail_scr)

    x = x_ref[...]
    h = _rmsnorm(x, gmix_ref[...]).astype(BF16)
    dt_raw = jnp.dot(h, wdt_ref[...], preferred_element_type=F32) + dtb_ref[...]
    dt = jnp.maximum(dt_raw, 0.0) + jnp.log(1.0 + jnp.exp(-jnp.abs(dt_raw)))
    a = -jnp.exp(alog_ref[...])
    row = lax.broadcasted_iota(jnp.int32, (L, L), 0)
    col = lax.broadcasted_iota(jnp.int32, (L, L), 1)
    causal = row >= col
    tril = causal.astype(F32)
    acs = jnp.dot(tril, dt * a, preferred_element_type=F32, precision=HIGHEST)
    acs_t = acs.T
    dt_t = dt.T
    exp_acs = jnp.exp(acs)
    tail = jnp.exp(acs[L - 1:L, :] - acs) * dt
    both = jnp.concatenate([exp_acs, tail], axis=0)
    both_e = jnp.dot(both, expand_ref[...], preferred_element_type=F32, precision=HIGHEST)
    exp_acs_e = both_e[:L]
    tail_e = both_e[L:]

    zx = zx_ref[...]
    z = zx[:, :SSD_D_INNER].astype(F32)
    xbc = zx[:, SSD_D_INNER:].astype(F32)
    ext = jnp.concatenate([tail_scr[...], xbc], axis=0)
    tail_scr[...] = xbc[L - SUBLANES:, :]
    conv = convb_ref[...] + convw_ref[SSD_CONV_WIDTH - 1:SSD_CONV_WIDTH, :] * xbc
    for k in range(SSD_CONV_WIDTH - 1):
        off = SUBLANES - (SSD_CONV_WIDTH - 1) + k
        conv = conv + convw_ref[k:k + 1, :] * ext[off:off + L, :]
    xbc = _silu(conv)
    xs = xbc[:, :SSD_D_INNER]
    bm = xbc[:, SSD_D_INNER:SSD_D_INNER + G * N]
    cm = xbc[:, SSD_D_INNER + G * N:]

    xs_tail = (xs * tail_e).astype(BF16)
    xs_b = xs.astype(BF16)
    y_parts = []
    for g in range(G):
        b_g = bm[:, g * N:(g + 1) * N].astype(BF16)
        c_g = cm[:, g * N:(g + 1) * N].astype(BF16)
        cb = lax.dot_general(c_g, b_g, (((1,), (1,)), ((), ())), preferred_element_type=F32)
        st = state_scr[g]
        y_inter = jnp.dot(c_g, st.astype(BF16), preferred_element_type=F32)
        y_g = y_inter * exp_acs_e[:, g * GD:(g + 1) * GD]
        intra = []
        for j in range(SSD_HEADS_PER_GROUP):
            hh = g * SSD_HEADS_PER_GROUP + j
            seg = acs[:, hh:hh + 1] - acs_t[hh:hh + 1, :]
            decay = jnp.exp(jnp.where(causal, seg, -jnp.inf))
            w = (cb * decay * dt_t[hh:hh + 1, :]).astype(BF16)
            intra.append(jnp.dot(w, xs_b[:, hh * P:(hh + 1) * P], preferred_element_type=F32))
        y_g = y_g + jnp.concatenate(intra, axis=1)
        y_parts.append(y_g)
        upd = lax.dot_general(b_g, xs_tail[:, g * GD:(g + 1) * GD], (((0,), (0,)), ((), ())),
                              preferred_element_type=F32)
        state_scr[g] = st * exp_acs_e[L - 1:L, g * GD:(g + 1) * GD] + upd
    y = jnp.concatenate(y_parts, axis=1)
    y = y + xs * dskip_ref[...]
    y = y * _silu(z)
    normed = []
    for g in range(G):
        y_g = y[:, g * GD:(g + 1) * GD]
        normed.append(_rmsnorm(y_g, gnorm_ref[:, g * GD:(g + 1) * GD]))
    yn = jnp.concatenate(normed, axis=1).astype(BF16)
    o_ref[...] = x + jnp.dot(yn, wout_ref[...], preferred_element_type=F32)


def _pad_lanes(v):
    return jnp.pad(v.astype(F32), (0, LANES - v.shape[0])).reshape(1, LANES)


def _ssd_layer(x, bsz, g_mix, w_in, conv_w, conv_b, dt_bias, a_log, d_skip, norm_g, w_out):
    t, d = x.shape
    nc = t // bsz // CHUNK
    w_zx = w_in[:, :SSD_D_INNER + SSD_CONV_DIM].astype(BF16)
    w_dt = jnp.pad(w_in[:, SSD_D_INNER + SSD_CONV_DIM:], ((0, 0), (0, LANES - SSD_N_HEADS))).astype(BF16)
    zx = _norm_matmul(x, g_mix, w_zx)
    head_of_lane = jnp.arange(SSD_D_INNER) // SSD_HEAD_DIM
    expand = (jnp.arange(LANES)[:, None] == head_of_lane[None, :]).astype(F32)
    dskip_e = jnp.repeat(d_skip.astype(F32), SSD_HEAD_DIM).reshape(1, SSD_D_INNER)
    const = lambda shape: pl.BlockSpec(shape, lambda b, c: (0,) * len(shape))
    tok = lambda width: pl.BlockSpec((CHUNK, width), lambda b, c: (b * nc + c, 0))
    return pl.pallas_call(
        _ssd_kernel,
        grid=(bsz, nc),
        in_specs=[tok(d), tok(SSD_D_INNER + SSD_CONV_DIM), const((1, d)), const((d, LANES)),
                  const((1, LANES)), const((1, LANES)), const((SSD_CONV_WIDTH, SSD_CONV_DIM)),
                  const((1, SSD_CONV_DIM)), const((1, SSD_D_INNER)), const((LANES, SSD_D_INNER)),
                  const((1, SSD_D_INNER)), const((SSD_D_INNER, d))],
        out_specs=tok(d),
        out_shape=jax.ShapeDtypeStruct((t, d), F32),
        scratch_shapes=[pltpu.VMEM((SSD_N_GROUPS, SSD_D_STATE, SSD_GROUP_DIM), F32),
                        pltpu.VMEM((SUBLANES, SSD_CONV_DIM), F32)],
        compiler_params=_params("parallel", "arbitrary"),
        name="ssd_scan",
    )(x, zx, g_mix.reshape(1, d), w_dt, _pad_lanes(dt_bias), _pad_lanes(a_log), conv_w.astype(F32),
      conv_b.reshape(1, -1).astype(F32), dskip_e, expand, norm_g.reshape(1, -1).astype(F32),
      w_out.astype(BF16))


HGRN_BLOCK = SUBLANES
HGRN_PAIRS = CHUNK * HGRN_BLOCK
HGRN_HEADS_PER_STEP = 8


def _split3(x):
    x1 = x.astype(BF16)
    r1 = x - x1.astype(F32)
    x2 = r1.astype(BF16)
    x3 = (r1 - x2.astype(F32)).astype(BF16)
    return x1, x2, x3


def _hgrn_kernel(q_ref, f_ref, i_ref, g_ref, lb_ref, gnorm_ref, rep_ref, tile_ref, dif_ref, fold_ref,
                 x_ref, wout_ref, o_ref, state_scr):
    L, C = CHUNK, HGRN_BLOCK
    NB = L // C
    K = HGRN_HEAD_DIM
    H = state_scr.shape[0]
    W = H * K
    nt = (((1,), (1,)), ((), ()))

    @pl.when(pl.program_id(2) == 0)
    def _():
        state_scr[...] = jnp.zeros_like(state_scr)

    lb = lb_ref[...]
    fl = f_ref[...].astype(F32)
    logf = jnp.log(lb + (1.0 - lb) * _sigmoid(fl))
    kk = (1.0 - lb) * _sigmoid(-fl)
    q = q_ref[...].astype(F32)
    row = lax.broadcasted_iota(jnp.int32, (L, L), 0)
    col = lax.broadcasted_iota(jnp.int32, (L, L), 1)
    b = jnp.dot((row >= col).astype(F32), logf, preferred_element_type=F32, precision=HIGHEST)
    b3 = b.reshape(NB, C, W)
    b_end3 = jnp.broadcast_to(b3[:, C - 1:C, :], b3.shape)
    b_prev3 = jnp.concatenate([jnp.zeros((1, C, W), F32), b_end3[:NB - 1]], axis=0)
    b_end = b_end3.reshape(L, W)
    b_prev = b_prev3.reshape(L, W)
    q_dec = (q * jnp.exp(b - b_prev)).astype(BF16)
    k_dec = (kk * jnp.exp(b_end - b)).astype(BF16)
    blk_dec = jnp.exp(b_end - b_prev)

    v_t = i_ref[...].astype(F32).T
    lane_blk = lax.broadcasted_iota(jnp.int32, (K, L), 1) // C
    inter = []
    for h in range(H):
        cols = slice(h * K, (h + 1) * K)
        v_h, k_h, q_h = v_t[cols], k_dec[:, cols], q_dec[:, cols]
        outer = [jnp.dot(jnp.where(lane_blk == i, v_h, 0.0).astype(BF16), k_h, preferred_element_type=F32)
                 for i in range(NB)]
        s = state_scr[h]
        o_h = jnp.zeros((K, L), F32)
        for i in range(NB):
            r = lax.dot_general(s.astype(BF16), q_h, nt, preferred_element_type=F32)
            o_h = jnp.where(lane_blk == i, r, o_h)
            s = s * blk_dec[i * C:i * C + 1, cols] + outer[i]
        state_scr[h] = s
        inter.append(o_h)

    q_e = jnp.dot(q.T.astype(BF16), rep_ref[...], preferred_element_type=F32)
    k_e = jnp.dot(kk.T.astype(BF16), tile_ref[...], preferred_element_type=F32)
    dif = dif_ref[...]
    seg = sum(jnp.dot(part, dif, preferred_element_type=F32) for part in _split3(b.T))
    j = lax.broadcasted_iota(jnp.int32, (W, HGRN_PAIRS), 1)
    ordered = (j % C) <= ((j // C) % C)
    prod = q_e * k_e * jnp.exp(jnp.where(ordered, seg, -jnp.inf))
    v_e = jnp.dot(v_t.astype(BF16), tile_ref[...], preferred_element_type=F32)
    weighted = [v_e[h * K:(h + 1) * K] * jnp.sum(prod[h * K:(h + 1) * K], axis=0, keepdims=True)
                for h in range(H)]
    o_t = jnp.concatenate(inter, axis=0) + jnp.dot(jnp.concatenate(weighted, axis=0).astype(BF16),
                                                   fold_ref[...], preferred_element_type=F32)
    o = o_t.T
    gate = _silu(g_ref[...].astype(F32))
    gnorm = gnorm_ref[...]
    normed = [_rmsnorm(o[:, h * K:(h + 1) * K], gnorm) * gate[:, h * K:(h + 1) * K] for h in range(H)]
    mixed = jnp.concatenate(normed, axis=1).astype(BF16)
    o_ref[...] = x_ref[...] + jnp.dot(mixed, wout_ref[...], preferred_element_type=F32)


def _hgrn_pair_maps():
    c = HGRN_BLOCK
    j = jnp.arange(HGRN_PAIRS)
    t_of = j // c
    s_of = (t_of // c) * c + j % c
    t = jnp.arange(CHUNK)[:, None]
    rep = (t == t_of[None, :]).astype(F32)
    tile = (t == s_of[None, :]).astype(F32)
    return rep.astype(BF16), tile.astype(BF16), (rep - tile).astype(BF16), rep.T.astype(BF16)


def _hgrn_layer(x, bsz, g_mix, w_in, lower_bound, norm_g, w_out):
    t, d = x.shape
    nc = t // bsz // CHUNK
    hd = HGRN_HEAD_DIM
    qfig = _norm_matmul(x, g_mix, w_in.astype(BF16))
    hs = HGRN_HEADS_PER_STEP
    groups = HGRN_N_HEADS // hs
    part = lambda k: pl.BlockSpec((CHUNK, hs * hd), lambda b, h, c: (b * nc + c, k * groups + h))
    const = lambda shape: pl.BlockSpec(shape, lambda b, h, c: (0, 0))
    rep, tile, dif, fold = _hgrn_pair_maps()
    assert groups == 1, "the fused out projection needs every head in the step"
    rows = pl.BlockSpec((CHUNK, d), lambda b, h, c: (b * nc + c, 0))
    return pl.pallas_call(
        _hgrn_kernel,
        grid=(bsz, groups, nc),
        in_specs=[part(0), part(1), part(2), part(3),
                  pl.BlockSpec((1, hs * hd), lambda b, h, c: (0, h)), const((1, hd)),
                  const(rep.shape), const(tile.shape), const(dif.shape), const(fold.shape),
                  rows, const((d, d))],
        out_specs=rows,
        out_shape=jax.ShapeDtypeStruct((t, d), F32),
        scratch_shapes=[pltpu.VMEM((hs, hd, hd), F32)],
        compiler_params=_params("parallel", "parallel", "arbitrary"),
        name="hgrn_scan",
    )(qfig, qfig, qfig, qfig, lower_bound.reshape(1, d).astype(F32), norm_g.reshape(1, hd).astype(F32),
      rep, tile, dif, fold, x, w_out.astype(BF16))


def _drain(steps):
    try:
        while True:
            next(steps)
    except StopIteration as done:
        return done.value


def _topk_rows_steps(s, k, payload=None):
    r = s.shape[0]
    rows = lax.broadcasted_iota(jnp.int32, s.shape, 0).astype(F32)
    vals, picks = [], []
    for _ in range(k):
        m = jnp.max(s, axis=0, keepdims=True)
        at_max = jnp.where(s == m, rows, float(r))
        pos = jnp.min(at_max, axis=0, keepdims=True)
        hit = at_max == pos
        vals.append(m)
        picks.append(pos if payload is None else jnp.sum(jnp.where(hit, payload, 0.0), axis=0, keepdims=True))
        s = jnp.where(hit, -jnp.inf, s)
        yield
    return jnp.concatenate(vals, axis=0), jnp.concatenate(picks, axis=0)


def _topk_rows(s, k, payload=None):
    return _drain(_topk_rows_steps(s, k, payload))


def _route_head_steps(keys_ref, qt_ref, idx_scr, gate_scr, head):
    top_s, top_i = [], []
    for c in range(2):
        hc = head * 2 + c
        q_hc = qt_ref[pl.ds(pl.multiple_of(hc * PEER_D_HALF, PEER_D_HALF), PEER_D_HALF), :]
        sc = jnp.dot(keys_ref[hc], q_hc, preferred_element_type=F32)
        s, i = yield from _topk_rows_steps(sc, PEER_TOPK)
        top_s.append(s)
        top_i.append(i)
    cand_s = _product_candidates(top_s[0], top_s[1], lambda a, b: a + b)
    cand_i = _product_candidates(top_i[0], top_i[1], lambda a, b: a * float(PEER_N_KEYS) + b)
    best_s, best_i = yield from _topk_rows_steps(cand_s, PEER_TOPK, payload=cand_i)
    e = jnp.exp(best_s - best_s[0:1, :])
    out_rows = pl.ds(pl.multiple_of(head * PEER_TOPK, PEER_TOPK), PEER_TOPK)
    idx_scr[out_rows, :] = best_i
    gate_scr[out_rows, :] = e / jnp.sum(e, axis=0, keepdims=True)


def _product_candidates(first, second, combine):
    half = PEER_TOPK // 2
    parts = [combine(first[0:1, :], second[0:half, :]), combine(first[0:1, :], second[half:, :])]
    parts += [combine(first[a:a + 1, :], second[0:half, :]) for a in range(1, half)]
    parts.append(combine(first[half:, :], second[0:1, :]))
    return jnp.concatenate(parts, axis=0)


def _peer_route_kernel(x_ref, g_ref, wq_ref, keys_ref, h_ref, idx_ref, gate_ref):
    h = _rmsnorm(x_ref[...], g_ref[...])
    h_ref[...] = h
    q_t = lax.dot_general(wq_ref[...], h.astype(BF16), (((1,), (1,)), ((), ())),
                          preferred_element_type=F32)
    idx_rows, gate_rows = [], []
    for head in range(PEER_N_HEADS):
        top_s, top_i = [], []
        for c in range(2):
            hc = head * 2 + c
            q_hc = q_t[hc * PEER_D_HALF:(hc + 1) * PEER_D_HALF, :].astype(BF16)
            sc = jnp.dot(keys_ref[hc], q_hc, preferred_element_type=F32)
            s, i = _topk_rows(sc, PEER_TOPK)
            top_s.append(s)
            top_i.append(i)
        cand_s = _product_candidates(top_s[0], top_s[1], lambda a, b: a + b)
        cand_i = _product_candidates(top_i[0], top_i[1], lambda a, b: a * float(PEER_N_KEYS) + b)
        best_s, best_i = _topk_rows(cand_s, PEER_TOPK, payload=cand_i)
        e = jnp.exp(best_s - best_s[0:1, :])
        gate_rows.append(e / jnp.sum(e, axis=0, keepdims=True))
        idx_rows.append(best_i)
    idx_ref[...] = jnp.concatenate(idx_rows, axis=0).T.astype(jnp.int32) * ROW_SUB
    gate_ref[...] = jnp.concatenate(gate_rows, axis=0).T


def _peer_route(x, g, wq_t, keys, n_tokens, tn=128):
    t, d = x.shape
    t = n_tokens
    nq = wq_t.shape[0]
    return pl.pallas_call(
        _peer_route_kernel,
        grid=(t // tn,),
        in_specs=[pl.BlockSpec((tn, d), lambda i: (i, 0)),
                  pl.BlockSpec((1, d), lambda i: (0, 0)),
                  pl.BlockSpec((nq, d), lambda i: (0, 0)),
                  pl.BlockSpec(keys.shape, lambda i: (0, 0, 0))],
        out_specs=[pl.BlockSpec((tn, d), lambda i: (i, 0)),
                   pl.BlockSpec((tn, PEER_SLOTS), lambda i: (i, 0)),
                   pl.BlockSpec((tn, PEER_SLOTS), lambda i: (i, 0))],
        out_shape=[jax.ShapeDtypeStruct((t, d), F32),
                   jax.ShapeDtypeStruct((t, PEER_SLOTS), jnp.int32),
                   jax.ShapeDtypeStruct((t, PEER_SLOTS), F32)],
        compiler_params=_params("parallel"),
        name="peer_route",
    )(x, g.reshape(1, d), wq_t, keys)


SLAB_ROWS = PEER_SLOTS * ROW_SUB
SLAB_COLS = 2 * SLAB_ROWS
GATHER_UNROLL = 16


def _pack_table(t):
    b = lax.bitcast_convert_type(t.astype(BF16), jnp.uint16).astype(jnp.uint32)
    w = b[:, :ROW_WORDS] | (b[:, ROW_WORDS:] << 16)
    return w.reshape(-1, LANES)


def _slab_maps():
    col = jnp.arange(SLAB_COLS)
    row = jnp.arange(SUBLANES)[:, None]
    diag = (row == ((col // 2) % ROW_SUB + ROW_SUB * (col % 2))[None, :]).astype(F32)
    spread = (jnp.arange(PEER_SLOTS)[:, None] == (col // SUBLANES)[None, :]).astype(BF16)
    return diag, spread


SIDE_EVERY = 32


def _gather_rows(idx_ref, tab_ref, slab, t, side=None):
    base = t * PEER_SLOTS
    for p in range(PEER_SLOTS):
        row = pl.multiple_of(idx_ref[base + p], ROW_SUB)
        slab[p * ROW_SUB:(p + 1) * ROW_SUB, :] = tab_ref[pl.ds(row, ROW_SUB), :]
        if side is not None and p % SIDE_EVERY == SIDE_EVERY - 1:
            next(side, None)


def _pipelined_tokens(tb, gather, finish, slab_a, slab_b, unroll=GATHER_UNROLL, side_work=None):
    gather(slab_a, 0)
    slabs = (slab_a, slab_b)

    def group(i, carry):
        t0 = unroll * i
        side = None if side_work is None else side_work(i)
        for k in range(unroll):
            gather(slabs[(k + 1) % 2], jnp.minimum(t0 + k + 1, tb - 1), side)
            finish(slabs[k % 2], t0 + k)
        if side is not None:
            _drain(side)
        return carry

    lax.fori_loop(0, tb // unroll, group, 0)


_NT = (((1,), (1,)), ((), ()))


def _peer_u_kernel(idx_ref, h_ref, gate_ref, tab_ref, diag_ref, spread_ref, spread_t_ref, crep_all_ref,
                   crep_ref, slab_a, slab_b, rbuf, side_work=None):
    tb = h_ref.shape[0]
    diag = diag_ref[...]

    def finish(slab, t):
        w = pltpu.bitcast(slab[...], BF16)
        r = lax.dot_general(h_ref[t].astype(BF16), w, _NT, preferred_element_type=F32)
        rbuf[pl.ds(t, 1), :] = jnp.sum(r * diag, axis=0, keepdims=True)

    _pipelined_tokens(tb, functools.partial(_gather_rows, idx_ref, tab_ref), finish, slab_a, slab_b,
                      side_work=side_work)
    spread_t = spread_t_ref[...]
    act = sum(jnp.dot(part, spread_t, preferred_element_type=F32) for part in _split3(rbuf[...]))
    gelu = 0.5 * act * (1.0 + lax.erf(act * (2.0 ** -0.5)))
    c = (gate_ref[...] * gelu).astype(BF16)
    crep_ref[...] = jnp.dot(c, spread_ref[...], preferred_element_type=F32)


def _peer_u_route_kernel(idx_ref, h_ref, gate_ref, tab_ref, diag_ref, spread_ref, spread_t_ref, crep_all_ref,
                         x2_ref, g_ref, wq_ref, keys_ref, crep_ref, h2_ref, idx2_ref, gate2_ref,
                         slab_a, slab_b, rbuf, qt_scr, idx_scr, gate_scr):
    h2 = _rmsnorm(x2_ref[...], g_ref[...])
    h2_ref[...] = h2
    qt_scr[...] = lax.dot_general(wq_ref[...], h2.astype(BF16), _NT, preferred_element_type=F32).astype(BF16)
    _peer_u_kernel(idx_ref, h_ref, gate_ref, tab_ref, diag_ref, spread_ref, spread_t_ref, crep_all_ref,
                   crep_ref, slab_a, slab_b, rbuf,
                   side_work=functools.partial(_route_head_steps, keys_ref, qt_scr, idx_scr, gate_scr))
    idx2_ref[...] = idx_scr[...].T.astype(jnp.int32) * ROW_SUB
    gate2_ref[...] = gate_scr[...].T


def _peer_v_kernel(idx_ref, crep_ref, x_ref, tab_ref, diag_ref, outg_ref, o_ref, slab_a, slab_b, *, out_norm):
    tb = x_ref.shape[0]
    diag = diag_ref[...]

    def finish(slab, t):
        w = pltpu.bitcast(slab[...], BF16)
        c = (crep_ref[pl.ds(t, 1), :] * diag).astype(BF16)
        y = jnp.dot(c, w, preferred_element_type=F32)
        flat = jnp.concatenate([y[s:s + 1, :] for s in range(SUBLANES)], axis=1)
        o_ref[pl.ds(t, 1), :] = x_ref[pl.ds(t, 1), :] + flat

    _pipelined_tokens(tb, functools.partial(_gather_rows, idx_ref, tab_ref), finish, slab_a, slab_b)
    if out_norm:
        o_ref[...] = _rmsnorm(o_ref[...], outg_ref[...])


PEER_TB = PEER_N_HEADS * GATHER_UNROLL
PEER_PARTS = 8


def _peer_layer(x, g, w_query, sub_keys, expert_u, expert_v, out_norm_g=None):
    t, d = x.shape
    tb = PEER_TB
    nb = t // tb
    parts = PEER_PARTS if nb % PEER_PARTS == 0 else 2
    npb = nb // parts
    tp = npb * tb
    wq_t = w_query.T.astype(BF16)
    nq = wq_t.shape[0]
    keys = sub_keys.reshape(2 * PEER_N_HEADS, PEER_N_KEYS, PEER_D_HALF).astype(BF16)
    u_packed, v_packed = _pack_table(expert_u), _pack_table(expert_v)
    diag, spread = _slab_maps()
    spread_t = spread.T

    tok = pl.BlockSpec((tb, d), lambda i: (i, 0))
    tok3 = pl.BlockSpec((tb, SUBLANES, LANES), lambda i: (i, 0, 0))
    tok2 = pl.BlockSpec((tb, PEER_SLOTS), lambda i: (i, 0))
    rep2 = pl.BlockSpec((tb, SLAB_COLS), lambda i: (i, 0))
    smem1 = pl.BlockSpec((tb * PEER_SLOTS,), lambda i: (i,), memory_space=pltpu.SMEM)
    table = pl.BlockSpec(u_packed.shape, lambda i: (0, 0), pipeline_mode=pl.Buffered(1))
    const = lambda a: pl.BlockSpec(a.shape, lambda i: (0,) * a.ndim)
    slab = pltpu.VMEM((SLAB_ROWS, LANES), jnp.uint32)
    rbuf = pltpu.VMEM((tb, SLAB_COLS), F32)
    g2 = g.reshape(1, d)

    whole = pl.BlockSpec(memory_space=pl.ANY)
    crep_shape = jax.ShapeDtypeStruct((t, SLAB_COLS), F32)
    part_shapes = [jax.ShapeDtypeStruct((tp, d), F32), jax.ShapeDtypeStruct((tp, PEER_SLOTS), jnp.int32),
                   jax.ShapeDtypeStruct((tp, PEER_SLOTS), F32)]
    u_specs = [smem1, tok3, tok2, table, const(diag), const(spread), const(spread_t), whole]
    crep = jnp.zeros((t, SLAB_COLS), F32)
    h, rows, gate = _peer_route(x, g, wq_t, keys, tp, tn=tb)
    all_rows = [rows]
    for j in range(parts):
        crep_spec = pl.BlockSpec((tb, SLAB_COLS), lambda i, j=j: (i + j * npb, 0))
        u_args = (rows.reshape(-1), h.reshape(tp, SUBLANES, LANES), gate, u_packed, diag, spread, spread_t, crep)
        if j + 1 < parts:
            crep, h, rows, gate = pl.pallas_call(
                _peer_u_route_kernel,
                grid=(npb,),
                in_specs=u_specs + [pl.BlockSpec((tb, d), lambda i, j=j: (i + (j + 1) * npb, 0)),
                                    const(g2), const(wq_t), const(keys)],
                out_specs=[crep_spec, tok, tok2, tok2],
                out_shape=[crep_shape] + part_shapes,
                scratch_shapes=[slab, slab, rbuf, pltpu.VMEM((nq, tb), BF16),
                                pltpu.VMEM((PEER_SLOTS, tb), F32), pltpu.VMEM((PEER_SLOTS, tb), F32)],
                input_output_aliases={len(u_specs) - 1: 0},
                compiler_params=_params("arbitrary"),
                name="peer_u_route",
            )(*u_args, x, g2, wq_t, keys)
            all_rows.append(rows)
        else:
            crep = pl.pallas_call(
                _peer_u_kernel,
                grid=(npb,),
                in_specs=u_specs,
                out_specs=crep_spec,
                out_shape=crep_shape,
                scratch_shapes=[slab, slab, rbuf],
                input_output_aliases={len(u_specs) - 1: 0},
                compiler_params=_params("arbitrary"),
                name="peer_u",
            )(*u_args)
    out_g = (g if out_norm_g is None else out_norm_g).reshape(1, d)
    return pl.pallas_call(
        functools.partial(_peer_v_kernel, out_norm=out_norm_g is not None),
        grid=(nb,),
        in_specs=[smem1, rep2, tok, table, const(diag), const(out_g)],
        out_specs=tok,
        out_shape=jax.ShapeDtypeStruct((t, d), F32),
        scratch_shapes=[slab, slab],
        compiler_params=_params("arbitrary"),
        name="peer_v",
    )(jnp.concatenate(all_rows, axis=0).reshape(-1), crep, x, v_packed, diag, out_g)


def _hgrn_lower_bound(lb_logits, layer):
    p = jax.nn.softmax(lb_logits.astype(F32), axis=0)
    return jnp.cumsum(p, axis=0)[layer] - p[0]


def kernel(x, mix_norm_g, ffn_norm_g, final_norm_g, ssd_w_in, ssd_conv_w, ssd_conv_b, ssd_dt_bias, ssd_a_log, ssd_d_skip, ssd_norm_g, ssd_w_out, hgrn_w_in, hgrn_lb_logits, hgrn_norm_g, hgrn_w_out, peer_w_query, peer_sub_keys, peer_u, peer_v):
    bsz, s, d = x.shape
    depth = mix_norm_g.shape[0]
    xf = x.reshape(bsz * s, d)
    for layer in range(depth):
        j = layer // 2
        if layer % 2 == 0:
            xf = _ssd_layer(xf, bsz, mix_norm_g[layer], ssd_w_in[j], ssd_conv_w[j], ssd_conv_b[j],
                            ssd_dt_bias[j], ssd_a_log[j], ssd_d_skip[j], ssd_norm_g[j], ssd_w_out[j])
        else:
            lb = _hgrn_lower_bound(hgrn_lb_logits, layer)
            xf = _hgrn_layer(xf, bsz, mix_norm_g[layer], hgrn_w_in[j], lb, hgrn_norm_g[j], hgrn_w_out[j])
        xf = _peer_layer(xf, ffn_norm_g[layer], peer_w_query[layer], peer_sub_keys[layer],
                         peer_u[layer], peer_v[layer],
                         out_norm_g=final_norm_g if layer == depth - 1 else None)
    return xf.reshape(bsz, s, d)
```
